```python
import jax
import jax.numpy as jnp
from jax import lax
import numpy as np

D_MODEL = 2048
BATCH = 8
SEQ = 2048
DEPTH = 2

HEAD_DIM = 128
ROPE_THETA = 10000.0
ATTN_BLOCK = 128
EPS = 1e-5

DN_HEADS = 8
DN_DK = 128
DN_DV = 128
DN_CONV = 5
DN_CHUNK = 64

MLA_HEADS = 8
MLA_Q_RANK = 512
MLA_KV_RANK = 512
MLA_NOPE = 128
MLA_ROPE = 64
MLA_V = 128

DIL_GROUPS = ((128, 1), (512, 4), (2048, 16))
DIL_HEADS = 8

PEER_HEADS = 8
PEER_N_KEYS = 128
PEER_N_EXPERTS = PEER_N_KEYS * PEER_N_KEYS
PEER_QUERY_DIM = 256
PEER_TOPK = 16
PEER_TOKEN_BLOCK = 128

N_EVEN = (DEPTH + 1) // 2
N_ODD = DEPTH // 2
DEEPNORM_ALPHA = (2 * DEPTH) ** 0.25
DEEPNORM_BETA = (8 * DEPTH) ** -0.25

EVEN_COLS = (DN_HEADS * DN_DK, DN_HEADS * DN_DK, DN_HEADS * DN_DV, DN_HEADS * DN_DV, 4 * DN_HEADS, MLA_Q_RANK, MLA_KV_RANK, MLA_ROPE)
EVEN_IN = sum(EVEN_COLS)
EVEN_OUT = DN_HEADS * DN_DV + MLA_HEADS * MLA_V
ODD_IN = len(DIL_GROUPS) * 3 * DIL_HEADS * HEAD_DIM
ODD_OUT = DIL_HEADS * HEAD_DIM

kernel_name = 'hybrid_deltanet_mla_dilated_peer_encoder'


def layer_norm(x, g, b):
    xf = x.astype(jnp.float32)
    mu = jnp.mean(xf, axis=-1, keepdims=True)
    var = jnp.mean(jnp.square(xf - mu), axis=-1, keepdims=True)
    return ((xf - mu) * lax.rsqrt(var + EPS) * g.astype(jnp.float32) + b.astype(jnp.float32)).astype(x.dtype)


def rms_norm(x, g):
    xf = x.astype(jnp.float32)
    return (xf * lax.rsqrt(jnp.mean(jnp.square(xf), axis=-1, keepdims=True) + EPS) * g.astype(jnp.float32)).astype(x.dtype)


def l2_normalize(x):
    return x * lax.rsqrt(jnp.sum(jnp.square(x), axis=-1, keepdims=True) + 1e-6)


def rope_tables(seq, dim):
    inv_freq = ROPE_THETA ** (-jnp.arange(0, dim, 2, dtype=jnp.float32) / dim)
    ang = jnp.arange(seq, dtype=jnp.float32)[:, None] * inv_freq[None, :]
    return jnp.cos(ang), jnp.sin(ang)


def apply_rope(x, cos, sin):
    x1, x2 = jnp.split(x.astype(jnp.float32), 2, axis=-1)
    c, s = cos[:, None, :], sin[:, None, :]
    return jnp.concatenate([x1 * c - x2 * s, x2 * c + x1 * s], axis=-1).astype(x.dtype)


def centred_depthwise_conv(x, w):
    width = w.shape[0]
    return lax.conv_general_dilated(
        x, w[:, None, :].astype(x.dtype), window_strides=(1,),
        padding=[((width - 1) // 2, width // 2)],
        dimension_numbers=('NWC', 'WIO', 'NWC'), feature_group_count=x.shape[-1])


def gated_delta_rule(q, k, v, g, beta):
    bsz, seq, nh, dk = q.shape
    dv = v.shape[-1]
    n_chunks = seq // DN_CHUNK
    c = DN_CHUNK

    def chunks(t):
        return t.reshape(bsz, n_chunks, c, nh, -1).transpose(0, 3, 1, 2, 4)

    q, k, v = chunks(q), chunks(k), chunks(v)
    g = jnp.cumsum(g.reshape(bsz, n_chunks, c, nh).transpose(0, 3, 1, 2), axis=-1)
    beta = beta.reshape(bsz, n_chunks, c, nh).transpose(0, 3, 1, 2)
    lower = jnp.tril(jnp.ones((c, c), dtype=bool))
    strict = jnp.tril(jnp.ones((c, c), dtype=jnp.float32), -1)
    decay = jnp.exp(jnp.where(lower, g[..., :, None] - g[..., None, :], -jnp.inf))
    k_beta = k * beta[..., None]
    v_beta = v * beta[..., None]
    l_mat = jnp.einsum('bhnid,bhnjd->bhnij', k_beta, k) * decay * strict
    a_mat = jnp.eye(c, dtype=jnp.float32) + l_mat
    rhs = jnp.concatenate([v_beta, k_beta * jnp.exp(g)[..., None]], axis=-1)
    sol = lax.linalg.triangular_solve(a_mat, rhs, left_side=True, lower=True, unit_diagonal=True)
    u, w = sol[..., :dv], sol[..., dv:]
    intra = jnp.einsum('bhnid,bhnjd->bhnij', q, k) * decay

    def step(state, inp):
        qc, kc, uc, wc, gc, ac = inp
        v_new = uc - wc @ state
        out = (qc * jnp.exp(gc)[..., None]) @ state + ac @ v_new
        g_last = gc[..., -1]
        k_dec = kc * jnp.exp(g_last[..., None] - gc)[..., None]
        state = state * jnp.exp(g_last)[..., None, None] + jnp.einsum('bhcd,bhce->bhde', k_dec, v_new)
        return state, out

    xs = tuple(jnp.moveaxis(t, 2, 0) for t in (q, k, u, w, g, intra))
    state0 = jnp.zeros((bsz, nh, dk, dv), jnp.float32)
    _, o = lax.scan(step, state0, xs)
    return o.transpose(1, 0, 3, 2, 4).reshape(bsz, seq, nh, dv)


def gated_deltanet(q, k, v, z, gate_in, conv_w, a_log_f, dt_bias_f, a_log_b, dt_bias_b, norm_g):
    bsz, seq, _ = q.shape
    f32 = jnp.float32
    qkv = jax.nn.silu(centred_depthwise_conv(jnp.concatenate([q, k, v], axis=-1), conv_w))
    q, k, v = jnp.split(qkv, [DN_HEADS * DN_DK, 2 * DN_HEADS * DN_DK], axis=-1)

    def heads(t, d):
        return t.reshape(bsz, seq, DN_HEADS, d).astype(f32)

    q = l2_normalize(heads(q, DN_DK)) * DN_DK ** -0.5
    k = l2_normalize(heads(k, DN_DK))
    v = heads(v, DN_DV)
    a_f, b_f, a_b, b_b = jnp.split(gate_in.astype(f32), 4, axis=-1)
    g_f = -jnp.exp(a_log_f.astype(f32)) * jax.nn.softplus(a_f + dt_bias_f.astype(f32))
    g_b = -jnp.exp(a_log_b.astype(f32)) * jax.nn.softplus(a_b + dt_bias_b.astype(f32))
    o_fwd = gated_delta_rule(q, k, v, g_f, jax.nn.sigmoid(b_f))

    def rev(t):
        return jnp.flip(t, axis=1)

    o_bwd = rev(gated_delta_rule(rev(q), rev(k), rev(v), rev(g_b), rev(jax.nn.sigmoid(b_b))))
    o = rms_norm(o_fwd + o_bwd, norm_g) * jax.nn.silu(heads(z, DN_DV))
    return o.reshape(bsz, seq, DN_HEADS * DN_DV).astype(z.dtype)


def blocked_softmax_attention(q, k, v, scale):
    bsz, seq, nh, dq = q.shape
    nb = seq // ATTN_BLOCK
    qb = q.reshape(bsz, nb, ATTN_BLOCK, nh, dq).transpose(1, 0, 2, 3, 4)

    def one_block(qblk):
        s = jnp.einsum('bqhd,bkhd->bhqk', qblk, k).astype(jnp.float32) * scale
        p = jax.nn.softmax(s, axis=-1)
        return jnp.einsum('bhqk,bkhd->bqhd', p.astype(v.dtype), v)

    o = lax.map(one_block, qb)
    return o.transpose(1, 0, 2, 3, 4).reshape(bsz, seq, nh, v.shape[-1])


def latent_attention(c_q, c_kv, k_rope, q_norm_g, w_uq, kv_norm_g, w_ukv, cos, sin):
    bsz, seq, _ = c_q.shape
    q = (rms_norm(c_q, q_norm_g) @ w_uq).reshape(bsz, seq, MLA_HEADS, MLA_NOPE + MLA_ROPE)
    q_nope, q_pe = q[..., :MLA_NOPE], apply_rope(q[..., MLA_NOPE:], cos, sin)
    kv = (rms_norm(c_kv, kv_norm_g) @ w_ukv).reshape(bsz, seq, MLA_HEADS, MLA_NOPE + MLA_V)
    k_nope, v = kv[..., :MLA_NOPE], kv[..., MLA_NOPE:]
    k_pe = apply_rope(k_rope[:, :, None, :], cos, sin)
    q = jnp.concatenate([q_nope, q_pe], axis=-1)
    k = jnp.concatenate([k_nope, jnp.broadcast_to(k_pe, (bsz, seq, MLA_HEADS, MLA_ROPE))], axis=-1)
    o = blocked_softmax_attention(q, k, v, (MLA_NOPE + MLA_ROPE) ** -0.5)
    return o.reshape(bsz, seq, MLA_HEADS * MLA_V)


def dilated_window_attention(q, k, v, dilation, radius):
    bsz, seq, nh, dh = q.shape
    length = seq // dilation
    blk = radius
    nb = -(-length // blk)
    lp = nb * blk
    z = bsz * dilation

    def strided(t):
        return t.reshape(bsz, length, dilation, nh, dh).transpose(0, 2, 1, 3, 4).reshape(z, length, nh, dh)

    qs, ks, vs = strided(q), strided(k), strided(v)
    qb = jnp.pad(qs, ((0, 0), (0, lp - length), (0, 0), (0, 0))).reshape(z, nb, blk, nh, dh)

    def band(t):
        tp = jnp.pad(t, ((0, 0), (blk, lp - length + blk), (0, 0), (0, 0))).reshape(z, nb + 2, blk, nh, dh)
        return jnp.concatenate([tp[:, :-2], tp[:, 1:-1], tp[:, 2:]], axis=2)

    kb, vb = band(ks), band(vs)
    qi = jnp.arange(nb)[:, None] * blk + jnp.arange(blk)[None, :]
    kj = jnp.arange(nb)[:, None] * blk - blk + jnp.arange(3 * blk)[None, :]
    mask = (jnp.abs(qi[:, :, None] - kj[:, None, :]) <= radius) & (kj[:, None, :] >= 0) & (kj[:, None, :] < length)
    s = jnp.einsum('znqhd,znkhd->znhqk', qb, kb).astype(jnp.float32) * dh ** -0.5
    s = jnp.where(mask[None, :, None], s, -jnp.inf)
    m = jnp.max(s, axis=-1, keepdims=True)
    p = jnp.exp(s - m)
    den = jnp.sum(p, axis=-1, keepdims=True)
    o = jnp.einsum('znhqk,znkhd->znqhd', (p / den).astype(v.dtype), vb)
    lse = (m + jnp.log(den))[..., 0].transpose(0, 1, 3, 2).reshape(z, lp, nh)[:, :length]
    o = o.reshape(z, lp, nh, dh)[:, :length]
    o = o.reshape(bsz, dilation, length, nh, dh).transpose(0, 2, 1, 3, 4).reshape(bsz, seq, nh, dh)
    lse = lse.reshape(bsz, dilation, length, nh).transpose(0, 2, 1, 3).reshape(bsz, seq, nh)
    return o, lse


def dilated_attention(h, cos, sin):
    bsz, seq, _ = h.shape
    outs, lses = [], []
    for (window, dilation), hg in zip(DIL_GROUPS, jnp.split(h, len(DIL_GROUPS), axis=-1)):
        q, k, v = [t.reshape(bsz, seq, DIL_HEADS, HEAD_DIM) for t in jnp.split(hg, 3, axis=-1)]
        o, lse = dilated_window_attention(apply_rope(q, cos, sin), apply_rope(k, cos, sin), v, dilation, window // (2 * dilation))
        outs.append(o.astype(jnp.float32))
        lses.append(lse)
    weights = jax.nn.softmax(jnp.stack(lses), axis=0)
    o = jnp.sum(weights[..., None] * jnp.stack(outs), axis=0)
    return o.reshape(bsz, seq, DIL_HEADS * HEAD_DIM).astype(h.dtype)


def peer(x, w_q, sub_keys, u, v):
    bsz, seq, d = x.shape
    t = bsz * seq
    kk = PEER_TOPK
    xt = x.reshape(t, d)
    qh = (xt @ w_q).reshape(t, PEER_HEADS, 2, PEER_QUERY_DIM // 2).astype(jnp.float32)
    s = jnp.einsum('thpd,pnd->thpn', qh, sub_keys.astype(jnp.float32))
    top_s, top_i = lax.top_k(s, kk)
    cand_s = (top_s[:, :, 0, :, None] + top_s[:, :, 1, None, :]).reshape(t, PEER_HEADS, kk * kk)
    cand_e = (top_i[:, :, 0, :, None] * PEER_N_KEYS + top_i[:, :, 1, None, :]).reshape(t, PEER_HEADS, kk * kk)
    best_s, best_j = lax.top_k(cand_s, kk)
    experts = jnp.take_along_axis(cand_e, best_j, axis=-1)
    gates = jax.nn.softmax(best_s, axis=-1)
    nblk = t // PEER_TOKEN_BLOCK

    def block(args):
        xb, eb, gb = args
        hid = jnp.einsum('td,thkd->thk', xb, u[eb]).astype(jnp.float32)
        act = (gb * jax.nn.gelu(hid, approximate=False)).astype(xb.dtype)
        return jnp.einsum('thk,thkd->td', act, v[eb])

    y = lax.map(block, (xt.reshape(nblk, PEER_TOKEN_BLOCK, d),
                        experts.reshape(nblk, PEER_TOKEN_BLOCK, PEER_HEADS, kk),
                        gates.reshape(nblk, PEER_TOKEN_BLOCK, PEER_HEADS, kk)))
    return y.reshape(bsz, seq, d)


def setup_inputs(seed: int = 0) -> dict:
    key = jax.random.key(seed)
    ks = iter(jax.random.split(key, 32))
    f32 = jnp.float32
    d = D_MODEL

    def normal(shape, scale):
        return jax.random.normal(next(ks), shape, f32) * scale

    def gain(shape):
        return 1.0 + normal(shape, 0.02)

    def a_log(shape):
        return jnp.log(jax.random.uniform(next(ks), shape, f32, 1.0, 16.0))

    def dt_bias(shape):
        dt = jax.random.uniform(next(ks), shape, f32, 1e-3, 1e-1)
        return dt + jnp.log(-jnp.expm1(-dt))

    return {
        'x': normal((BATCH, SEQ, d), 1.0),
        'ab_w_in': normal((N_EVEN, d, EVEN_IN), d ** -0.5),
        'a_conv_w': normal((N_EVEN, DN_CONV, 2 * DN_HEADS * DN_DK + DN_HEADS * DN_DV), DN_CONV ** -0.5),
        'a_log_f': a_log((N_EVEN, DN_HEADS)),
        'a_dt_bias_f': dt_bias((N_EVEN, DN_HEADS)),
        'a_log_b': a_log((N_EVEN, DN_HEADS)),
        'a_dt_bias_b': dt_bias((N_EVEN, DN_HEADS)),
        'a_out_norm_g': gain((N_EVEN, DN_DV)),
        'b_q_norm_g': gain((N_EVEN, MLA_Q_RANK)),
        'b_w_uq': normal((N_EVEN, MLA_Q_RANK, MLA_HEADS * (MLA_NOPE + MLA_ROPE)), MLA_Q_RANK ** -0.5),
        'b_kv_norm_g': gain((N_EVEN, MLA_KV_RANK)),
        'b_w_ukv': normal((N_EVEN, MLA_KV_RANK, MLA_HEADS * (MLA_NOPE + MLA_V)), MLA_KV_RANK ** -0.5),
        'ab_w_out': normal((N_EVEN, EVEN_OUT, d), DEEPNORM_BETA * EVEN_OUT ** -0.5),
        'c_w_in': normal((N_ODD, d, ODD_IN), d ** -0.5),
        'c_w_out': normal((N_ODD, ODD_OUT, d), DEEPNORM_BETA * ODD_OUT ** -0.5),
        'mix_ln_g': gain((DEPTH, d)),
        'mix_ln_b': normal((DEPTH, d), 0.02),
        'peer_w_q': normal((DEPTH, d, PEER_HEADS * PEER_QUERY_DIM), d ** -0.5),
        'peer_sub_keys': normal((DEPTH, 2, PEER_N_KEYS, PEER_QUERY_DIM // 2), (PEER_QUERY_DIM // 2) ** -0.5),
        'peer_u': normal((DEPTH, PEER_N_EXPERTS, d), d ** -0.5),
        'peer_v': normal((DEPTH, PEER_N_EXPERTS, d), DEEPNORM_BETA),
        'ffn_ln_g': gain((DEPTH, d)),
        'ffn_ln_b': normal((DEPTH, d), 0.02),
    }


def reference(x, ab_w_in, a_conv_w, a_log_f, a_dt_bias_f, a_log_b, a_dt_bias_b, a_out_norm_g,
              b_q_norm_g, b_w_uq, b_kv_norm_g, b_w_ukv, ab_w_out, c_w_in, c_w_out,
              mix_ln_g, mix_ln_b, peer_w_q, peer_sub_keys, peer_u, peer_v, ffn_ln_g, ffn_ln_b):
    seq = x.shape[1]
    cos_h, sin_h = rope_tables(seq, HEAD_DIM)
    cos_m, sin_m = rope_tables(seq, MLA_ROPE)
    even_split = [int(c) for c in np.cumsum(EVEN_COLS)[:-1]]
    for layer in range(DEPTH):
        i = layer // 2
        if layer % 2 == 0:
            h = x @ ab_w_in[i]
            dq, dk, dv, dz, dgate, cq, ckv, kr = jnp.split(h, even_split, axis=-1)
            o_a = gated_deltanet(dq, dk, dv, dz, dgate, a_conv_w[i], a_log_f[i], a_dt_bias_f[i],
                                 a_log_b[i], a_dt_bias_b[i], a_out_norm_g[i])
            o_b = latent_attention(cq, ckv, kr, b_q_norm_g[i], b_w_uq[i], b_kv_norm_g[i], b_w_ukv[i], cos_m, sin_m)
            mix = jnp.concatenate([o_a, o_b], axis=-1) @ ab_w_out[i]
        else:
            mix = dilated_attention(x @ c_w_in[i], cos_h, sin_h) @ c_w_out[i]
        x = layer_norm(DEEPNORM_ALPHA * x + mix, mix_ln_g[layer], mix_ln_b[layer])
        ffn = peer(x, peer_w_q[layer], peer_sub_keys[layer], peer_u[layer], peer_v[layer])
        x = layer_norm(DEEPNORM_ALPHA * x + ffn, ffn_ln_g[layer], ffn_ln_b[layer])
    return x
```

```python
import functools

import jax
import jax.numpy as jnp
from jax import lax
from jax.experimental import pallas as pl
from jax.experimental.pallas import tpu as pltpu

F32 = jnp.float32
BF16 = jnp.bfloat16
HIGHEST = lax.Precision.HIGHEST

HEAD_DIM = 128
ROPE_THETA = 10000.0
EPS = 1e-5
LANES = 128

DN_HEADS = 8
DN_DK = 128
DN_DV = 128
DN_CONV = 5
DN_CHUNK = 64

MLA_HEADS = 8
MLA_Q_RANK = 512
MLA_KV_RANK = 512
MLA_NOPE = 128
MLA_ROPE = 64
MLA_V = 128

DIL_GROUPS = ((128, 1), (512, 4), (2048, 16))
DIL_HEADS = 8

PEER_HEADS = 8
PEER_N_KEYS = 128
PEER_QUERY_DIM = 256
PEER_TOPK = 16
PEER_SEL = PEER_HEADS * PEER_TOPK

VMEM_LIMIT = 48 * 1024 * 1024


def _params(*sem):
    return pltpu.CompilerParams(dimension_semantics=sem, vmem_limit_bytes=VMEM_LIMIT)


def _dot_nt(a, b, **kw):
    return lax.dot_general(a, b, (((1,), (1,)), ((), ())), preferred_element_type=F32, **kw)


def _mm_kernel(a_ref, b_ref, o_ref):
    o_ref[...] = jnp.dot(a_ref[...].astype(BF16), b_ref[...], preferred_element_type=F32).astype(o_ref.dtype)


def matmul(a, b, *, tm=512, tn=512, out_dtype=F32):
    m, k = a.shape
    n = b.shape[1]
    return pl.pallas_call(
        _mm_kernel,
        grid=(n // tn, m // tm),
        in_specs=[pl.BlockSpec((tm, k), lambda j, i: (i, 0)), pl.BlockSpec((k, tn), lambda j, i: (0, j))],
        out_specs=pl.BlockSpec((tm, tn), lambda j, i: (i, j)),
        out_shape=jax.ShapeDtypeStruct((m, n), out_dtype),
        compiler_params=_params("parallel", "parallel"),
        name="matmul",
    )(a, b)


def _mm2_kernel(a1_ref, b1_ref, a2_ref, b2_ref, o_ref):
    acc = jnp.dot(a1_ref[...].astype(BF16), b1_ref[...], preferred_element_type=F32)
    acc += jnp.dot(a2_ref[...].astype(BF16), b2_ref[...], preferred_element_type=F32)
    o_ref[...] = acc


def matmul2(a1, b1, a2, b2, *, tm=512, tn=512):
    m, k1 = a1.shape
    k2 = a2.shape[1]
    n = b1.shape[1]
    return pl.pallas_call(
        _mm2_kernel,
        grid=(n // tn, m // tm),
        in_specs=[
            pl.BlockSpec((tm, k1), lambda j, i: (i, 0)),
            pl.BlockSpec((k1, tn), lambda j, i: (0, j)),
            pl.BlockSpec((tm, k2), lambda j, i: (i, 0)),
            pl.BlockSpec((k2, tn), lambda j, i: (0, j)),
        ],
        out_specs=pl.BlockSpec((tm, tn), lambda j, i: (i, j)),
        out_shape=jax.ShapeDtypeStruct((m, n), F32),
        compiler_params=_params("parallel", "parallel"),
        name="matmul2",
    )(a1, b1, a2, b2)


def _mm_nt_kernel(a_ref, b_ref, o_ref):
    o_ref[...] = _dot_nt(a_ref[...], b_ref[...].astype(BF16))


def matmul_nt(a, b, *, tm=512, tn=512):
    m, k = a.shape
    n = b.shape[0]
    return pl.pallas_call(
        _mm_nt_kernel,
        grid=(m // tm, n // tn),
        in_specs=[pl.BlockSpec((tm, k), lambda i, j: (i, 0)), pl.BlockSpec((tn, k), lambda i, j: (j, 0))],
        out_specs=pl.BlockSpec((tm, tn), lambda i, j: (i, j)),
        out_shape=jax.ShapeDtypeStruct((m, n), F32),
        compiler_params=_params("parallel", "parallel"),
        name="matmul_nt",
    )(a, b)


def _rms_mm_kernel(a_ref, g_ref, b_ref, o_ref):
    a = a_ref[...]
    a = a * lax.rsqrt(jnp.mean(a * a, axis=-1, keepdims=True) + EPS) * g_ref[...]
    o_ref[...] = jnp.dot(a.astype(BF16), b_ref[...], preferred_element_type=F32)


def rms_matmul(a, col_block, g, b, *, tm=512, tn=512):
    m = a.shape[0]
    k, n = b.shape
    return pl.pallas_call(
        _rms_mm_kernel,
        grid=(n // tn, m // tm),
        in_specs=[
            pl.BlockSpec((tm, k), lambda j, i: (i, col_block)),
            pl.BlockSpec((1, k), lambda j, i: (0, 0)),
            pl.BlockSpec((k, tn), lambda j, i: (0, j)),
        ],
        out_specs=pl.BlockSpec((tm, tn), lambda j, i: (i, j)),
        out_shape=jax.ShapeDtypeStruct((m, n), F32),
        compiler_params=_params("parallel", "parallel"),
        name="rms_matmul",
    )(a, g.reshape(1, k), b)


def _add_ln_kernel(x_ref, y_ref, g_ref, b_ref, o_ref, *, alpha):
    z = alpha * x_ref[...] + y_ref[...]
    mu = jnp.mean(z, axis=-1, keepdims=True)
    zc = z - mu
    var = jnp.mean(zc * zc, axis=-1, keepdims=True)
    o_ref[...] = zc * lax.rsqrt(var + EPS) * g_ref[...] + b_ref[...]


def add_layer_norm(x, y, g, b, alpha, *, tm=256):
    m, d = x.shape
    return pl.pallas_call(
        functools.partial(_add_ln_kernel, alpha=alpha),
        grid=(m // tm,),
        in_specs=[
            pl.BlockSpec((tm, d), lambda i: (i, 0)),
            pl.BlockSpec((tm, d), lambda i: (i, 0)),
            pl.BlockSpec((1, d), lambda i: (0, 0)),
            pl.BlockSpec((1, d), lambda i: (0, 0)),
        ],
        out_specs=pl.BlockSpec((tm, d), lambda i: (i, 0)),
        out_shape=jax.ShapeDtypeStruct((m, d), F32),
        compiler_params=_params("parallel"),
        name="add_layer_norm",
    )(x, y, g.reshape(1, d), b.reshape(1, d))


def _rope_tables(seq, dim):
    half = dim // 2
    inv_freq = ROPE_THETA ** (-jnp.arange(0, dim, 2, dtype=F32) / dim)
    ang = jnp.arange(seq, dtype=F32)[:, None] * inv_freq[None, :]
    cos, sin = jnp.cos(ang), jnp.sin(ang)
    reps = LANES // dim
    cos_t = jnp.tile(jnp.concatenate([cos, cos], axis=-1), (1, reps))
    sin_t = jnp.tile(jnp.concatenate([-sin, sin], axis=-1), (1, reps))
    return cos_t, sin_t


def _rope(x, cos_t, sin_t, half):
    if 2 * half == LANES:
        partner = pltpu.roll(x, half, 1)
    else:
        lane = lax.broadcasted_iota(jnp.int32, x.shape, 1)
        first = (lane % (2 * half)) < half
        partner = jnp.where(first, pltpu.roll(x, LANES - half, 1), pltpu.roll(x, half, 1))
    return x * cos_t + partner * sin_t


def _conv_kernel(x_ref, w_ref, o_ref, pad_ref, *, seq, width, n_norm, n_scaled, scale):
    c = pl.program_id(1)
    half = (width - 1) // 2
    pad = 8
    pad_ref[0:pad, :] = jnp.zeros((pad, LANES), F32)
    pad_ref[pad + seq:pad + seq + pad, :] = jnp.zeros((pad, LANES), F32)
    pad_ref[pad:pad + seq, :] = x_ref[0]
    acc = jnp.zeros((seq, LANES), F32)
    for j in range(width):
        acc += w_ref[j:j + 1, :] * pad_ref[pl.ds(pad + j - half, seq), :]
    y = acc * jax.nn.sigmoid(acc)
    nrm = y * lax.rsqrt(jnp.sum(y * y, axis=-1, keepdims=True) + 1e-6)
    nrm = nrm * jnp.where(c < n_scaled, scale, 1.0)
    o_ref[0] = jnp.where(c < n_norm, nrm, y)


def deltanet_conv(h_all, conv_w, seq):
    bsz = h_all.shape[0]
    n_ch = conv_w.shape[1]
    width = conv_w.shape[0]
    return pl.pallas_call(
        functools.partial(_conv_kernel, seq=seq, width=width, n_norm=2 * DN_HEADS, n_scaled=DN_HEADS,
                          scale=DN_DK ** -0.5),
        grid=(bsz, n_ch // LANES),
        in_specs=[
            pl.BlockSpec((1, seq, LANES), lambda b, c: (b, 0, c)),
            pl.BlockSpec((width, LANES), lambda b, c: (0, c)),
        ],
        out_specs=pl.BlockSpec((1, seq, LANES), lambda b, c: (b, 0, c)),
        out_shape=jax.ShapeDtypeStruct((bsz, seq, n_ch), F32),
        scratch_shapes=[pltpu.VMEM((seq + 16, LANES), F32)],
        compiler_params=_params("parallel", "parallel"),
        name="deltanet_conv",
    )(h_all, conv_w)


def _delta_kernel(p_ref, q_ref, k_ref, v_ref, gt_ref, o_ref, s_ref, *, seq, chunk):
    h = pl.program_id(1)
    d = pl.program_id(2)
    n_chunks = seq // chunk
    ii = lax.broadcasted_iota(jnp.int32, (chunk, chunk), 0)
    jj = lax.broadcasted_iota(jnp.int32, (chunk, chunk), 1)
    diff = jnp.where(d == 0, ii - jj, jj - ii)
    lower = diff >= 0
    strict = diff > 0
    eye = ii == jj
    eye_f = eye.astype(F32)
    a_log = p_ref[h, 2 * d]
    dt_bias = p_ref[h, 2 * d + 1]
    neg_a = -jnp.exp(jnp.full((1, chunk), a_log, F32))
    s_ref[...] = jnp.zeros_like(s_ref)

    def body(n, carry):
        c = jnp.where(d == 0, n, n_chunks - 1 - n)
        r0 = pl.multiple_of(c * chunk, chunk)
        qc = q_ref[0, pl.ds(r0, chunk), :]
        kc = k_ref[0, pl.ds(r0, chunk), :]
        vc = v_ref[0, pl.ds(r0, chunk), :]
        a_row = gt_ref[0, 0, 2 * d, pl.ds(c, 1), :]
        b_row = gt_ref[0, 0, 2 * d + 1, pl.ds(c, 1), :]
        g_row = neg_a * jax.nn.softplus(a_row + dt_bias)
        beta_row = jax.nn.sigmoid(b_row)
        gc_col = jnp.sum(jnp.where(lower, g_row, 0.0), axis=1, keepdims=True)
        gc_row = jnp.sum(jnp.where(eye, gc_col, 0.0), axis=0, keepdims=True)
        beta_col = jnp.sum(jnp.where(eye, beta_row, 0.0), axis=1, keepdims=True)
        decay = jnp.where(lower, jnp.exp(jnp.minimum(gc_col - gc_row, 0.0)), 0.0)
        kb = kc * beta_col
        vb = vc * beta_col
        lmat = jnp.where(strict, _dot_nt(kb, kc) * decay, 0.0)
        xp = -lmat
        tinv = eye_f + xp
        span = 2
        while span < chunk:
            xp = jnp.dot(xp, xp, preferred_element_type=F32, precision=HIGHEST)
            tinv = tinv + jnp.dot(tinv, xp, preferred_element_type=F32, precision=HIGHEST)
            span *= 2
        egc = jnp.exp(gc_col)
        u = jnp.dot(tinv, vb, preferred_element_type=F32, precision=HIGHEST)
        w = jnp.dot(tinv, kb * egc, preferred_element_type=F32, precision=HIGHEST)
        intra = _dot_nt(qc, kc) * decay
        state = s_ref[...]
        v_new = u - jnp.dot(w, state, preferred_element_type=F32)
        out = jnp.dot(qc * egc, state, preferred_element_type=F32) + jnp.dot(intra, v_new, preferred_element_type=F32)
        g_last = jnp.sum(g_row, axis=1, keepdims=True)
        k_dec = kc * jnp.exp(g_last - gc_col)
        s_ref[...] = state * jnp.exp(g_last) + jnp.dot(k_dec.T, v_new, preferred_element_type=F32)
        o_ref[0, 0, pl.ds(r0, chunk), :] = out
        return carry

    lax.fori_loop(0, n_chunks, body, 0)


def deltanet_scan(qkv, gates, gate_params, seq):
    bsz = qkv.shape[0]
    nh = DN_HEADS
    chunk = DN_CHUNK
    n_chunks = seq // chunk
    return pl.pallas_call(
        functools.partial(_delta_kernel, seq=seq, chunk=chunk),
        grid=(bsz, nh, 2),
        in_specs=[
            pl.BlockSpec(memory_space=pltpu.SMEM),
            pl.BlockSpec((1, seq, LANES), lambda b, h, d: (b, 0, h)),
            pl.BlockSpec((1, seq, LANES), lambda b, h, d: (b, 0, nh + h)),
            pl.BlockSpec((1, seq, LANES), lambda b, h, d: (b, 0, 2 * nh + h)),
            pl.BlockSpec((1, 1, 4, n_chunks, chunk), lambda b, h, d: (b, h, 0, 0, 0)),
        ],
        out_specs=pl.BlockSpec((1, 1, seq, LANES), lambda b, h, d: (d, b, 0, h)),
        out_shape=jax.ShapeDtypeStruct((2, bsz, seq, nh * DN_DV), F32),
        scratch_shapes=[pltpu.VMEM((DN_DK, DN_DV), F32)],
        compiler_params=_params("parallel", "parallel", "parallel"),
        name="deltanet_scan",
    )(gate_params, qkv, qkv, qkv, gates)


def _gated_norm_kernel(of_ref, ob_ref, z_ref, g_ref, o_ref):
    o = of_ref[0, 0] + ob_ref[0, 0]
    o = o * lax.rsqrt(jnp.mean(o * o, axis=-1, keepdims=True) + EPS) * g_ref[...]
    z = z_ref[0]
    o_ref[0] = o * (z * jax.nn.sigmoid(z))


def deltanet_gated_norm(o2, h_all, z_block0, norm_g, seq, *, ts=512):
    bsz = o2.shape[1]
    nh = DN_HEADS
    return pl.pallas_call(
        _gated_norm_kernel,
        grid=(bsz, seq // ts, nh),
        in_specs=[
            pl.BlockSpec((1, 1, ts, LANES), lambda b, i, h: (0, b, i, h)),
            pl.BlockSpec((1, 1, ts, LANES), lambda b, i, h: (1, b, i, h)),
            pl.BlockSpec((1, ts, LANES), lambda b, i, h: (b, i, z_block0 + h)),
            pl.BlockSpec((1, LANES), lambda b, i, h: (0, 0)),
        ],
        out_specs=pl.BlockSpec((1, ts, LANES), lambda b, i, h: (b, i, h)),
        out_shape=jax.ShapeDtypeStruct((bsz, seq, nh * DN_DV), F32),
        compiler_params=_params("parallel", "parallel", "parallel"),
        name="deltanet_gated_norm",
    )(o2, o2, h_all, norm_g.reshape(1, DN_DV))


def _mla_attn_kernel(qn_ref, qpe_ref, kn_ref, kpe_ref, v_ref, cq_ref, sq_ref, ck_ref, sk_ref, o_ref, *, scale):
    h = pl.program_id(1)
    half = MLA_ROPE // 2
    qpe = _rope(qpe_ref[0], cq_ref[...], sq_ref[...], half)
    lane = lax.broadcasted_iota(jnp.int32, qpe.shape, 1)
    qpe = jnp.where((lane // MLA_ROPE) == (h % 2), qpe, 0.0)
    kpe = _rope(kpe_ref[0], ck_ref[...], sk_ref[...], half)
    s = _dot_nt(qn_ref[0].astype(BF16), kn_ref[0].astype(BF16))
    s += _dot_nt(qpe.astype(BF16), kpe.astype(BF16))
    s = s * scale
    m = jnp.max(s, axis=-1, keepdims=True)
    p = jnp.exp(s - m)
    den = jnp.sum(p, axis=-1, keepdims=True)
    o = jnp.dot(p.astype(BF16), v_ref[0].astype(BF16), preferred_element_type=F32)
    o_ref[0] = o / den


def mla_attention(q_up, kv_up, h_all, kpe_block, cos_t, sin_t, seq, *, tq=512):
    bsz = q_up.shape[0]
    nh = MLA_HEADS
    scale = (MLA_NOPE + MLA_ROPE) ** -0.5
    return pl.pallas_call(
        functools.partial(_mla_attn_kernel, scale=scale),
        grid=(bsz, nh, seq // tq),
        in_specs=[
            pl.BlockSpec((1, tq, LANES), lambda b, h, i: (b, i, h)),
            pl.BlockSpec((1, tq, LANES), lambda b, h, i: (b, i, nh + h // 2)),
            pl.BlockSpec((1, seq, LANES), lambda b, h, i: (b, 0, h)),
            pl.BlockSpec((1, seq, LANES), lambda b, h, i: (b, 0, kpe_block)),
            pl.BlockSpec((1, seq, LANES), lambda b, h, i: (b, 0, nh + h)),
            pl.BlockSpec((tq, LANES), lambda b, h, i: (i, 0)),
            pl.BlockSpec((tq, LANES), lambda b, h, i: (i, 0)),
            pl.BlockSpec((seq, LANES), lambda b, h, i: (0, 0)),
            pl.BlockSpec((seq, LANES), lambda b, h, i: (0, 0)),
        ],
        out_specs=pl.BlockSpec((1, tq, LANES), lambda b, h, i: (b, i, h)),
        out_shape=jax.ShapeDtypeStruct((bsz, seq, nh * MLA_V), F32),
        compiler_params=_params("parallel", "parallel", "parallel"),
        name="mla_attention",
    )(q_up, q_up, kv_up, h_all, kv_up, cos_t, sin_t, cos_t, sin_t)


def _dil_attn_kernel(q_ref, k_ref, v_ref, cq_ref, sq_ref, ck_ref, sk_ref, o_ref, lse_ref, *, radius, tq, scale):
    i = pl.program_id(3)
    half = HEAD_DIM // 2
    q = _rope(q_ref[0], cq_ref[...], sq_ref[...], half)
    k = _rope(k_ref[0], ck_ref[...], sk_ref[...], half)
    s = _dot_nt(q.astype(BF16), k.astype(BF16)) * scale
    qi = i * tq + lax.broadcasted_iota(jnp.int32, s.shape, 0)
    kj = lax.broadcasted_iota(jnp.int32, s.shape, 1)
    s = jnp.where(jnp.abs(qi - kj) <= radius, s, -jnp.inf)
    m = jnp.max(s, axis=-1, keepdims=True)
    p = jnp.exp(s - m)
    den = jnp.sum(p, axis=-1, keepdims=True)
    o = jnp.dot(p.astype(BF16), v_ref[0].astype(BF16), preferred_element_type=F32)
    o_ref[0] = o / den
    lse_ref[0] = jnp.broadcast_to(m + jnp.log(den), o.shape)


def dilated_group_attention(h3, group, dilation, radius, cos_t, sin_t, seq):
    bsz, _, cols = h3.shape
    nh = DIL_HEADS
    length = seq // dilation
    tq = min(length, 256)
    cb = cols // LANES
    hv = h3.reshape(bsz, length, dilation * cols)
    cv = cos_t.reshape(length, dilation * LANES)
    sv = sin_t.reshape(length, dilation * LANES)
    base = group * 3 * nh
    out_sds = jax.ShapeDtypeStruct((bsz, length, dilation * nh * HEAD_DIM), F32)
    o, lse = pl.pallas_call(
        functools.partial(_dil_attn_kernel, radius=radius, tq=tq, scale=HEAD_DIM ** -0.5),
        grid=(bsz, dilation, nh, length // tq),
        in_specs=[
            pl.BlockSpec((1, tq, LANES), lambda b, r, h, i: (b, i, r * cb + base + h)),
            pl.BlockSpec((1, length, LANES), lambda b, r, h, i: (b, 0, r * cb + base + nh + h)),
            pl.BlockSpec((1, length, LANES), lambda b, r, h, i: (b, 0, r * cb + base + 2 * nh + h)),
            pl.BlockSpec((tq, LANES), lambda b, r, h, i: (i, r)),
            pl.BlockSpec((tq, LANES), lambda b, r, h, i: (i, r)),
            pl.BlockSpec((length, LANES), lambda b, r, h, i: (0, r)),
            pl.BlockSpec((length, LANES), lambda b, r, h, i: (0, r)),
        ],
        out_specs=[
            pl.BlockSpec((1, tq, LANES), lambda b, r, h, i: (b, i, r * nh + h)),
            pl.BlockSpec((1, tq, LANES), lambda b, r, h, i: (b, i, r * nh + h)),
        ],
        out_shape=[out_sds, out_sds],
        compiler_params=_params("parallel", "parallel", "parallel", "parallel"),
        name=f"dilated_attention_g{group}",
    )(hv, hv, hv, cv, sv, cv, sv)
    return o.reshape(bsz * seq, nh * HEAD_DIM), lse.reshape(bsz * seq, nh * HEAD_DIM)


def _dil_combine_kernel(*refs):
    o_ref = refs[-1]
    n = (len(refs) - 1) // 2
    outs = [refs[2 * g][...] for g in range(n)]
    lses = [refs[2 * g + 1][...] for g in range(n)]
    m = functools.reduce(jnp.maximum, lses)
    es = [jnp.exp(l - m) for l in lses]
    num = functools.reduce(jnp.add, [e * o for e, o in zip(es, outs)])
    o_ref[...] = num / functools.reduce(jnp.add, es)


def dilated_combine(pairs, *, tm=256):
    m, d = pairs[0][0].shape
    flat = [a for pair in pairs for a in pair]
    spec = pl.BlockSpec((tm, d), lambda i: (i, 0))
    return pl.pallas_call(
        _dil_combine_kernel,
        grid=(m // tm,),
        in_specs=[spec] * len(flat),
        out_specs=spec,
        out_shape=jax.ShapeDtypeStruct((m, d), F32),
        compiler_params=_params("parallel"),
        name="dilated_combine",
    )(*flat)


def _top_rows(s, k, payload=None):
    rows = s.shape[0]
    idx = lax.broadcasted_iota(jnp.int32, s.shape, 0)
    vals, inds, pays = [], [], []
    for _ in range(k):
        m = jnp.max(s, axis=0, keepdims=True)
        sel = jnp.min(jnp.where(s == m, idx, rows), axis=0, keepdims=True)
        hit = idx == sel
        vals.append(m)
        inds.append(sel)
        if payload is not None:
            pays.append(jnp.sum(jnp.where(hit, payload, 0), axis=0, keepdims=True))
        s = jnp.where(hit, -jnp.inf, s)
    cat = lambda xs: jnp.concatenate(xs, axis=0)
    return cat(vals), cat(inds), (cat(pays) if payload is not None else None)


def _peer_select_kernel(qt_ref, keys_ref, e_ref, g_ref):
    kk = PEER_TOPK
    half = PEER_QUERY_DIM // 2
    tops = []
    for p in range(2):
        s = jnp.dot(keys_ref[p], qt_ref[p * half:(p + 1) * half, :], preferred_element_type=F32)
        tops.append(_top_rows(s, kk)[:2])
    (s0, i0), (s1, i1) = tops
    cand_s = jnp.concatenate([s0[a:a + 1] + s1 for a in range(kk)], axis=0)
    cand_e = jnp.concatenate([i0[a:a + 1] * PEER_N_KEYS + i1 for a in range(kk)], axis=0)
    best_s, _, experts = _top_rows(cand_s, kk, payload=cand_e)
    ex = jnp.exp(best_s - best_s[0:1])
    e_ref[0] = experts
    g_ref[0] = ex / jnp.sum(ex, axis=0, keepdims=True)


def peer_select(q_t, sub_keys, *, tt=128):
    t = q_t.shape[1]
    nh = PEER_HEADS
    kk = PEER_TOPK
    return pl.pallas_call(
        _peer_select_kernel,
        grid=(t // tt, nh),
        in_specs=[
            pl.BlockSpec((PEER_QUERY_DIM, tt), lambda i, h: (h, i)),
            pl.BlockSpec((2, PEER_N_KEYS, PEER_QUERY_DIM // 2), lambda i, h: (0, 0, 0)),
        ],
        out_specs=[
            pl.BlockSpec((1, kk, tt), lambda i, h: (h, 0, i)),
            pl.BlockSpec((1, kk, tt), lambda i, h: (h, 0, i)),
        ],
        out_shape=[jax.ShapeDtypeStruct((nh, kk, t), jnp.int32), jax.ShapeDtypeStruct((nh, kk, t), F32)],
        compiler_params=_params("parallel", "parallel"),
        name="peer_select",
    )(q_t, sub_keys)


def _erf(x):
    x = jnp.clip(x, -4.0, 4.0)
    x2 = x * x
    alpha = (-2.72614225801306e-10, 2.77068142495902e-08, -2.10102402082508e-06, -5.69250639462346e-05,
             -7.34990630326855e-04, -2.95459980854025e-03, -1.60960333262415e-02)
    beta = (-1.45660718464996e-05, -2.13374055278905e-04, -1.68282697438203e-03, -7.37332916720468e-03,
            -1.42647390514189e-02)
    p = jnp.full_like(x, alpha[0])
    for c in alpha[1:]:
        p = p * x2 + c
    q = jnp.full_like(x, beta[0])
    for c in beta[1:]:
        q = q * x2 + c
    return x * p / q


def _peer_expert_kernel(idx_hbm, g_ref, x_ref, u_hbm, v_hbm, y_ref, idx_smem, ubuf, vbuf, isem, usem, vsem, *, tb):
    n_sel = PEER_SEL
    step = pl.program_id(0)
    idx_copy = pltpu.make_async_copy(idx_hbm.at[pl.ds(step * (tb * n_sel), tb * n_sel)], idx_smem, isem)
    idx_copy.start()
    idx_copy.wait()

    def row_copy(tbl, buf, sem, slot, row, j):
        return pltpu.make_async_copy(tbl.at[pl.ds(row, 1), :], buf.at[slot, pl.ds(j, 1), :], sem.at[slot])

    def issue(t, slot):
        for j in range(n_sel):
            row = idx_smem[t * n_sel + j]
            row_copy(u_hbm, ubuf, usem, slot, row, j).start()
            row_copy(v_hbm, vbuf, vsem, slot, row, j).start()

    def wait(slot):
        for j in range(n_sel):
            row_copy(u_hbm, ubuf, usem, slot, 0, j).wait()
        for j in range(n_sel):
            row_copy(v_hbm, vbuf, vsem, slot, 0, j).wait()

    ii = lax.broadcasted_iota(jnp.int32, (n_sel, n_sel), 0)
    jj = lax.broadcasted_iota(jnp.int32, (n_sel, n_sel), 1)
    eye = ii == jj

    issue(0, 0)

    def body(t, carry):
        slot = t % 2

        @pl.when(t + 1 < tb)
        def _():
            issue(t + 1, 1 - slot)

        wait(slot)
        x_row = x_ref[pl.ds(t, 1), :]
        hid = jnp.sum(ubuf[slot] * x_row, axis=1, keepdims=True)
        g_col = jnp.sum(jnp.where(eye, g_ref[pl.ds(t, 1), :], 0.0), axis=1, keepdims=True)
        act = g_col * (0.5 * hid * (1.0 + _erf(hid * (2.0 ** -0.5))))
        y_ref[pl.ds(t, 1), :] = jnp.sum(vbuf[slot] * act, axis=0, keepdims=True)
        return carry

    lax.fori_loop(0, tb, body, 0)


def peer_experts(idx, gates, x, u, v, *, tb=64):
    t, d = x.shape
    n_sel = PEER_SEL
    return pl.pallas_call(
        functools.partial(_peer_expert_kernel, tb=tb),
        grid=(t // tb,),
        in_specs=[
            pl.BlockSpec(memory_space=pl.ANY),
            pl.BlockSpec((tb, n_sel), lambda i: (i, 0)),
            pl.BlockSpec((tb, d), lambda i: (i, 0)),
            pl.BlockSpec(memory_space=pl.ANY),
            pl.BlockSpec(memory_space=pl.ANY),
        ],
        out_specs=pl.BlockSpec((tb, d), lambda i: (i, 0)),
        out_shape=jax.ShapeDtypeStruct((t, d), F32),
        scratch_shapes=[
            pltpu.SMEM((tb * n_sel,), jnp.int32),
            pltpu.VMEM((2, n_sel, d), F32),
            pltpu.VMEM((2, n_sel, d), F32),
            pltpu.SemaphoreType.DMA,
            pltpu.SemaphoreType.DMA((2,)),
            pltpu.SemaphoreType.DMA((2,)),
        ],
        compiler_params=_params("arbitrary"),
        name="peer_experts",
    )(idx, gates, x, u, v)


def peer(x, w_q, sub_keys, u, v):
    t = x.shape[0]
    q_t = matmul_nt(w_q.T.astype(BF16), x)
    experts, gates = peer_select(q_t, sub_keys)
    idx = experts.reshape(PEER_SEL, t).T.reshape(t * PEER_SEL)
    gates = gates.reshape(PEER_SEL, t).T
    return peer_experts(idx, gates, x, u, v)


def _even_mixer(xt, bsz, seq, w_in, conv_w, gate_params, norm_g, q_norm_g, w_uq, kv_norm_g, w_ukv, w_out,
                cos_m, sin_m):
    nh = DN_HEADS
    c_qkvz = 4 * nh * DN_DK
    c_gate = c_qkvz + 4 * nh
    c_cq = c_gate + MLA_Q_RANK
    c_ckv = c_cq + MLA_KV_RANK
    w_gate = w_in[:, c_qkvz:c_gate]
    w_kr = w_in[:, c_ckv:]
    w_all = jnp.concatenate(
        [w_in[:, :c_qkvz], w_in[:, c_gate:c_ckv], w_kr, w_kr, w_gate,
         jnp.zeros((w_in.shape[0], LANES - 4 * nh), w_in.dtype)], axis=1).astype(BF16)
    h_all = matmul(xt, w_all, tn=768)
    cols = w_all.shape[1]
    h3d = h_all.reshape(bsz, seq, cols)
    lat0 = c_qkvz
    kpe_block = (lat0 + MLA_Q_RANK + MLA_KV_RANK) // LANES
    g0 = lat0 + MLA_Q_RANK + MLA_KV_RANK + LANES

    qkv = deltanet_conv(h3d, conv_w, seq)
    n_chunks = seq // DN_CHUNK
    gates = h_all[:, g0:g0 + 4 * nh].reshape(bsz, n_chunks, DN_CHUNK, 4, nh).transpose(0, 4, 3, 1, 2)
    o2 = deltanet_scan(qkv, gates, gate_params, seq)
    o_a = deltanet_gated_norm(o2, h3d, 3 * nh, norm_g, seq)

    hq = MLA_NOPE + MLA_ROPE
    w_uq_r = w_uq.reshape(MLA_Q_RANK, MLA_HEADS, hq)
    w_uq_p = jnp.concatenate([w_uq_r[:, :, :MLA_NOPE].reshape(MLA_Q_RANK, -1),
                              w_uq_r[:, :, MLA_NOPE:].reshape(MLA_Q_RANK, -1)], axis=1).astype(BF16)
    w_ukv_r = w_ukv.reshape(MLA_KV_RANK, MLA_HEADS, MLA_NOPE + MLA_V)
    w_ukv_p = jnp.concatenate([w_ukv_r[:, :, :MLA_NOPE].reshape(MLA_KV_RANK, -1),
                               w_ukv_r[:, :, MLA_NOPE:].reshape(MLA_KV_RANK, -1)], axis=1).astype(BF16)
    q_up = rms_matmul(h_all, lat0 // MLA_Q_RANK, q_norm_g, w_uq_p)
    kv_up = rms_matmul(h_all, (lat0 + MLA_Q_RANK) // MLA_KV_RANK, kv_norm_g, w_ukv_p)
    o_b = mla_attention(q_up.reshape(bsz, seq, -1), kv_up.reshape(bsz, seq, -1), h3d, kpe_block, cos_m, sin_m, seq)

    n_a = nh * DN_DV
    w_out = w_out.astype(BF16)
    return matmul2(o_a.reshape(bsz * seq, n_a), w_out[:n_a], o_b.reshape(bsz * seq, -1), w_out[n_a:])


def _odd_mixer(xt, bsz, seq, w_in, w_out, cos_h, sin_h):
    h3 = matmul(xt, w_in.astype(BF16), tn=768).reshape(bsz, seq, -1)
    pairs = []
    for g, (window, dilation) in enumerate(DIL_GROUPS):
        pairs.append(dilated_group_attention(h3, g, dilation, window // (2 * dilation), cos_h, sin_h, seq))
    o = dilated_combine(pairs)
    return matmul(o, w_out.astype(BF16))


def kernel(x, ab_w_in, a_conv_w, a_log_f, a_dt_bias_f, a_log_b, a_dt_bias_b, a_out_norm_g, b_q_norm_g, b_w_uq,
           b_kv_norm_g, b_w_ukv, ab_w_out, c_w_in, c_w_out, mix_ln_g, mix_ln_b, peer_w_q, peer_sub_keys, peer_u,
           peer_v, ffn_ln_g, ffn_ln_b):
    bsz, seq, d = x.shape
    depth = mix_ln_g.shape[0]
    alpha = (2 * depth) ** 0.25
    cos_h, sin_h = _rope_tables(seq, HEAD_DIM)
    cos_m, sin_m = _rope_tables(seq, MLA_ROPE)
    xt = x.reshape(bsz * seq, d)
    for layer in range(depth):
        i = layer // 2
        if layer % 2 == 0:
            gate_params = jnp.stack([a_log_f[i], a_dt_bias_f[i], a_log_b[i], a_dt_bias_b[i]], axis=1)
            mix = _even_mixer(xt, bsz, seq, ab_w_in[i], a_conv_w[i], gate_params, a_out_norm_g[i], b_q_norm_g[i],
                              b_w_uq[i], b_kv_norm_g[i], b_w_ukv[i], ab_w_out[i], cos_m, sin_m)
        else:
            mix = _odd_mixer(xt, bsz, seq, c_w_in[i], c_w_out[i], cos_h, sin_h)
        xt = add_layer_norm(xt, mix, mix_ln_g[layer], mix_ln_b[layer], alpha)
        ffn = peer(xt, peer_w_q[layer], peer_sub_keys[layer], peer_u[layer], peer_v[layer])
        xt = add_layer_norm(xt, ffn, ffn_ln_g[layer], ffn_ln_b[layer], alpha)
    return xt.reshape(bsz, seq, d)
```

```python
import functools

import jax
import jax.numpy as jnp
from jax import lax
from jax.experimental import pallas as pl
from jax.experimental.pallas import tpu as pltpu

F32 = jnp.float32
BF16 = jnp.bfloat16
HIGHEST = lax.Precision.HIGHEST

HEAD_DIM = 128
ROPE_THETA = 10000.0
EPS = 1e-5
LANES = 128

DN_HEADS = 8
DN_DK = 128
DN_DV = 128
DN_CONV = 5
DN_CHUNK = 64

MLA_HEADS = 8
MLA_Q_RANK = 512
MLA_KV_RANK = 512
MLA_NOPE = 128
MLA_ROPE = 64
MLA_V = 128

DIL_GROUPS = ((128, 1), (512, 4), (2048, 16))
DIL_HEADS = 8

PEER_HEADS = 8
PEER_N_KEYS = 128
PEER_QUERY_DIM = 256
PEER_TOPK = 16
PEER_SEL = PEER_HEADS * PEER_TOPK

VMEM_LIMIT = 48 * 1024 * 1024


def _params(*sem):
    return pltpu.CompilerParams(dimension_semantics=sem, vmem_limit_bytes=VMEM_LIMIT)


def _dot_nt(a, b, **kw):
    return lax.dot_general(a, b, (((1,), (1,)), ((), ())), preferred_element_type=F32, **kw)


def _mm_kernel(a_ref, b_ref, o_ref):
    o_ref[...] = jnp.dot(a_ref[...].astype(BF16), b_ref[...], preferred_element_type=F32).astype(o_ref.dtype)


def matmul(a, b, *, tm=512, tn=512, out_dtype=F32):
    m, k = a.shape
    n = b.shape[1]
    return pl.pallas_call(
        _mm_kernel,
        grid=(n // tn, m // tm),
        in_specs=[pl.BlockSpec((tm, k), lambda j, i: (i, 0)), pl.BlockSpec((k, tn), lambda j, i: (0, j))],
        out_specs=pl.BlockSpec((tm, tn), lambda j, i: (i, j)),
        out_shape=jax.ShapeDtypeStruct((m, n), out_dtype),
        compiler_params=_params("parallel", "parallel"),
        name="matmul",
    )(a, b)


def _mm2_kernel(a1_ref, b1_ref, a2_ref, b2_ref, o_ref):
    acc = jnp.dot(a1_ref[...].astype(BF16), b1_ref[...], preferred_element_type=F32)
    acc += jnp.dot(a2_ref[...].astype(BF16), b2_ref[...], preferred_element_type=F32)
    o_ref[...] = acc


def matmul2(a1, b1, a2, b2, *, tm=512, tn=512):
    m, k1 = a1.shape
    k2 = a2.shape[1]
    n = b1.shape[1]
    return pl.pallas_call(
        _mm2_kernel,
        grid=(n // tn, m // tm),
        in_specs=[
            pl.BlockSpec((tm, k1), lambda j, i: (i, 0)),
            pl.BlockSpec((k1, tn), lambda j, i: (0, j)),
            pl.BlockSpec((tm, k2), lambda j, i: (i, 0)),
            pl.BlockSpec((k2, tn), lambda j, i: (0, j)),
        ],
        out_specs=pl.BlockSpec((tm, tn), lambda j, i: (i, j)),
        out_shape=jax.ShapeDtypeStruct((m, n), F32),
        compiler_params=_params("parallel", "parallel"),
        name="matmul2",
    )(a1, b1, a2, b2)


def _mm_nt_kernel(a_ref, b_ref, o_ref):
    o_ref[...] = _dot_nt(a_ref[...], b_ref[...].astype(BF16))


def matmul_nt(a, b, *, tm=512, tn=512):
    m, k = a.shape
    n = b.shape[0]
    return pl.pallas_call(
        _mm_nt_kernel,
        grid=(m // tm, n // tn),
        in_specs=[pl.BlockSpec((tm, k), lambda i, j: (i, 0)), pl.BlockSpec((tn, k), lambda i, j: (j, 0))],
        out_specs=pl.BlockSpec((tm, tn), lambda i, j: (i, j)),
        out_shape=jax.ShapeDtypeStruct((m, n), F32),
        compiler_params=_params("parallel", "parallel"),
        name="matmul_nt",
    )(a, b)


def _rms_mm_kernel(a_ref, g_ref, b_ref, o_ref):
    a = a_ref[...]
    a = a * lax.rsqrt(jnp.mean(a * a, axis=-1, keepdims=True) + EPS) * g_ref[...]
    o_ref[...] = jnp.dot(a.astype(BF16), b_ref[...], preferred_element_type=F32)


def rms_matmul(a, col_block, g, b, *, tm=512, tn=512):
    m = a.shape[0]
    k, n = b.shape
    return pl.pallas_call(
        _rms_mm_kernel,
        grid=(n // tn, m // tm),
        in_specs=[
            pl.BlockSpec((tm, k), lambda j, i: (i, col_block)),
            pl.BlockSpec((1, k), lambda j, i: (0, 0)),
            pl.BlockSpec((k, tn), lambda j, i: (0, j)),
        ],
        out_specs=pl.BlockSpec((tm, tn), lambda j, i: (i, j)),
        out_shape=jax.ShapeDtypeStruct((m, n), F32),
        compiler_params=_params("parallel", "parallel"),
        name="rms_matmul",
    )(a, g.reshape(1, k), b)


def _add_ln_kernel(x_ref, y_ref, g_ref, b_ref, o_ref, *, alpha):
    z = alpha * x_ref[...] + y_ref[...]
    mu = jnp.mean(z, axis=-1, keepdims=True)
    zc = z - mu
    var = jnp.mean(zc * zc, axis=-1, keepdims=True)
    o_ref[...] = zc * lax.rsqrt(var + EPS) * g_ref[...] + b_ref[...]


def add_layer_norm(x, y, g, b, alpha, *, tm=256):
    m, d = x.shape
    return pl.pallas_call(
        functools.partial(_add_ln_kernel, alpha=alpha),
        grid=(m // tm,),
        in_specs=[
            pl.BlockSpec((tm, d), lambda i: (i, 0)),
            pl.BlockSpec((tm, d), lambda i: (i, 0)),
            pl.BlockSpec((1, d), lambda i: (0, 0)),
            pl.BlockSpec((1, d), lambda i: (0, 0)),
        ],
        out_specs=pl.BlockSpec((tm, d), lambda i: (i, 0)),
        out_shape=jax.ShapeDtypeStruct((m, d), F32),
        compiler_params=_params("parallel"),
        name="add_layer_norm",
    )(x, y, g.reshape(1, d), b.reshape(1, d))


def _rope_tables(seq, dim):
    half = dim // 2
    inv_freq = ROPE_THETA ** (-jnp.arange(0, dim, 2, dtype=F32) / dim)
    ang = jnp.arange(seq, dtype=F32)[:, None] * inv_freq[None, :]
    cos, sin = jnp.cos(ang), jnp.sin(ang)
    reps = LANES // dim
    cos_t = jnp.tile(jnp.concatenate([cos, cos], axis=-1), (1, reps))
    sin_t = jnp.tile(jnp.concatenate([-sin, sin], axis=-1), (1, reps))
    return cos_t, sin_t


def _rope(x, cos_t, sin_t, half):
    if 2 * half == LANES:
        partner = pltpu.roll(x, half, 1)
    else:
        lane = lax.broadcasted_iota(jnp.int32, x.shape, 1)
        first = (lane % (2 * half)) < half
        partner = jnp.where(first, pltpu.roll(x, LANES - half, 1), pltpu.roll(x, half, 1))
    return x * cos_t + partner * sin_t


def _conv_kernel(x_ref, w_ref, o_ref, pad_ref, *, seq, width, n_norm, n_scaled, scale):
    c = pl.program_id(1)
    half = (width - 1) // 2
    pad = 8
    pad_ref[0:pad, :] = jnp.zeros((pad, LANES), F32)
    pad_ref[pad + seq:pad + seq + pad, :] = jnp.zeros((pad, LANES), F32)
    pad_ref[pad:pad + seq, :] = x_ref[0]
    acc = jnp.zeros((seq, LANES), F32)
    for j in range(width):
        acc += w_ref[j:j + 1, :] * pad_ref[pl.ds(pad + j - half, seq), :]
    y = acc * jax.nn.sigmoid(acc)
    nrm = y * lax.rsqrt(jnp.sum(y * y, axis=-1, keepdims=True) + 1e-6)
    nrm = nrm * jnp.where(c < n_scaled, scale, 1.0)
    o_ref[0] = jnp.where(c < n_norm, nrm, y)


def deltanet_conv(h_all, conv_w, seq):
    bsz = h_all.shape[0]
    n_ch = conv_w.shape[1]
    width = conv_w.shape[0]
    return pl.pallas_call(
        functools.partial(_conv_kernel, seq=seq, width=width, n_norm=2 * DN_HEADS, n_scaled=DN_HEADS,
                          scale=DN_DK ** -0.5),
        grid=(bsz, n_ch // LANES),
        in_specs=[
            pl.BlockSpec((1, seq, LANES), lambda b, c: (b, 0, c)),
            pl.BlockSpec((width, LANES), lambda b, c: (0, c)),
        ],
        out_specs=pl.BlockSpec((1, seq, LANES), lambda b, c: (b, 0, c)),
        out_shape=jax.ShapeDtypeStruct((bsz, seq, n_ch), F32),
        scratch_shapes=[pltpu.VMEM((seq + 16, LANES), F32)],
        compiler_params=_params("parallel", "parallel"),
        name="deltanet_conv",
    )(h_all, conv_w)


def _delta_chunk(qc, kc, vc, a_row, b_row, neg_a, dt_bias, lower, strict, eye, state):
    chunk = qc.shape[0]
    g_row = neg_a * jax.nn.softplus(a_row + dt_bias)
    beta_row = jax.nn.sigmoid(b_row)
    gc_col = jnp.sum(jnp.where(lower, g_row, 0.0), axis=1, keepdims=True)
    gc_row = jnp.sum(jnp.where(eye, gc_col, 0.0), axis=0, keepdims=True)
    beta_col = jnp.sum(jnp.where(eye, beta_row, 0.0), axis=1, keepdims=True)
    decay = jnp.where(lower, jnp.exp(jnp.minimum(gc_col - gc_row, 0.0)), 0.0)
    kb = kc * beta_col
    vb = vc * beta_col
    lmat = jnp.where(strict, _dot_nt(kb, kc) * decay, 0.0)
    xp = -lmat
    tinv = eye.astype(F32) + xp
    span = 2
    while span < chunk:
        xp = jnp.dot(xp, xp, preferred_element_type=F32, precision=HIGHEST)
        tinv = tinv + jnp.dot(tinv, xp, preferred_element_type=F32, precision=HIGHEST)
        span *= 2
    egc = jnp.exp(gc_col)
    u = jnp.dot(tinv, vb, preferred_element_type=F32, precision=HIGHEST)
    w = jnp.dot(tinv, kb * egc, preferred_element_type=F32, precision=HIGHEST)
    intra = _dot_nt(qc, kc) * decay
    v_new = u - jnp.dot(w, state, preferred_element_type=F32)
    out = jnp.dot(qc * egc, state, preferred_element_type=F32) + jnp.dot(intra, v_new, preferred_element_type=F32)
    g_last = jnp.sum(g_row, axis=1, keepdims=True)
    k_dec = kc * jnp.exp(g_last - gc_col)
    new_state = state * jnp.exp(g_last) + jnp.dot(k_dec.T, v_new, preferred_element_type=F32)
    return out, new_state


def _delta_kernel(p_ref, q_ref, k_ref, v_ref, gt_ref, o_ref, s_ref, *, seq, chunk, heads_per_step):
    hg = pl.program_id(1)
    n_chunks = seq // chunk
    ii = lax.broadcasted_iota(jnp.int32, (chunk, chunk), 0)
    jj = lax.broadcasted_iota(jnp.int32, (chunk, chunk), 1)
    eye = ii == jj
    masks = ((ii >= jj, ii > jj), (ii <= jj, ii < jj))
    neg_a, dt_bias = [], []
    for hh in range(heads_per_step):
        for d in range(2):
            h = hg * heads_per_step + hh
            neg_a.append(-jnp.exp(jnp.full((1, chunk), p_ref[h, 2 * d], F32)))
            dt_bias.append(p_ref[h, 2 * d + 1])
    s_ref[...] = jnp.zeros_like(s_ref)

    def body(n, carry):
        for hh in range(heads_per_step):
            cols = slice(hh * LANES, (hh + 1) * LANES)
            for d in range(2):
                ch = 2 * hh + d
                c = n if d == 0 else n_chunks - 1 - n
                r0 = pl.multiple_of(c * chunk, chunk)
                out, new_state = _delta_chunk(
                    q_ref[0, pl.ds(r0, chunk), cols], k_ref[0, pl.ds(r0, chunk), cols], v_ref[0, pl.ds(r0, chunk), cols],
                    gt_ref[0, hh, 2 * d, pl.ds(c, 1), :], gt_ref[0, hh, 2 * d + 1, pl.ds(c, 1), :],
                    neg_a[ch], dt_bias[ch], masks[d][0], masks[d][1], eye, s_ref[ch])
                s_ref[ch] = new_state
                o_ref[d, 0, pl.ds(r0, chunk), cols] = out
        return carry

    lax.fori_loop(0, n_chunks, body, 0)


def deltanet_scan(qkv, gates, gate_params, seq, *, heads_per_step=2):
    bsz = qkv.shape[0]
    nh = DN_HEADS
    chunk = DN_CHUNK
    n_chunks = seq // chunk
    hps = heads_per_step
    groups = nh // hps
    width = hps * LANES
    return pl.pallas_call(
        functools.partial(_delta_kernel, seq=seq, chunk=chunk, heads_per_step=hps),
        grid=(bsz, groups),
        in_specs=[
            pl.BlockSpec(memory_space=pltpu.SMEM),
            pl.BlockSpec((1, seq, width), lambda b, g: (b, 0, g)),
            pl.BlockSpec((1, seq, width), lambda b, g: (b, 0, groups + g)),
            pl.BlockSpec((1, seq, width), lambda b, g: (b, 0, 2 * groups + g)),
            pl.BlockSpec((1, hps, 4, n_chunks, chunk), lambda b, g: (b, g, 0, 0, 0)),
        ],
        out_specs=pl.BlockSpec((2, 1, seq, width), lambda b, g: (0, b, 0, g)),
        out_shape=jax.ShapeDtypeStruct((2, bsz, seq, nh * DN_DV), F32),
        scratch_shapes=[pltpu.VMEM((2 * hps, DN_DK, DN_DV), F32)],
        compiler_params=_params("parallel", "parallel"),
        name="deltanet_scan",
    )(gate_params, qkv, qkv, qkv, gates)


def _gated_norm_kernel(of_ref, ob_ref, z_ref, g_ref, o_ref):
    o = of_ref[0, 0] + ob_ref[0, 0]
    o = o * lax.rsqrt(jnp.mean(o * o, axis=-1, keepdims=True) + EPS) * g_ref[...]
    z = z_ref[0]
    o_ref[0] = o * (z * jax.nn.sigmoid(z))


def deltanet_gated_norm(o2, h_all, z_block0, norm_g, seq, *, ts=512):
    bsz = o2.shape[1]
    nh = DN_HEADS
    return pl.pallas_call(
        _gated_norm_kernel,
        grid=(bsz, seq // ts, nh),
        in_specs=[
            pl.BlockSpec((1, 1, ts, LANES), lambda b, i, h: (0, b, i, h)),
            pl.BlockSpec((1, 1, ts, LANES), lambda b, i, h: (1, b, i, h)),
            pl.BlockSpec((1, ts, LANES), lambda b, i, h: (b, i, z_block0 + h)),
            pl.BlockSpec((1, LANES), lambda b, i, h: (0, 0)),
        ],
        out_specs=pl.BlockSpec((1, ts, LANES), lambda b, i, h: (b, i, h)),
        out_shape=jax.ShapeDtypeStruct((bsz, seq, nh * DN_DV), F32),
        compiler_params=_params("parallel", "parallel", "parallel"),
        name="deltanet_gated_norm",
    )(o2, o2, h_all, norm_g.reshape(1, DN_DV))


def _mla_attn_kernel(qn_ref, qpe_ref, kn_ref, kpe_ref, v_ref, cq_ref, sq_ref, ck_ref, sk_ref, o_ref, *, scale):
    h = pl.program_id(1)
    half = MLA_ROPE // 2
    qpe = _rope(qpe_ref[0], cq_ref[...], sq_ref[...], half)
    lane = lax.broadcasted_iota(jnp.int32, qpe.shape, 1)
    qpe = jnp.where((lane // MLA_ROPE) == (h % 2), qpe, 0.0)
    kpe = _rope(kpe_ref[0], ck_ref[...], sk_ref[...], half)
    s = _dot_nt(qn_ref[0].astype(BF16), kn_ref[0].astype(BF16))
    s += _dot_nt(qpe.astype(BF16), kpe.astype(BF16))
    s = s * scale
    m = jnp.max(s, axis=-1, keepdims=True)
    p = jnp.exp(s - m)
    den = jnp.sum(p, axis=-1, keepdims=True)
    o = jnp.dot(p.astype(BF16), v_ref[0].astype(BF16), preferred_element_type=F32)
    o_ref[0] = o / den


def mla_attention(q_up, kv_up, h_all, kpe_block, cos_t, sin_t, seq, *, tq=512):
    bsz = q_up.shape[0]
    nh = MLA_HEADS
    scale = (MLA_NOPE + MLA_ROPE) ** -0.5
    return pl.pallas_call(
        functools.partial(_mla_attn_kernel, scale=scale),
        grid=(bsz, nh, seq // tq),
        in_specs=[
            pl.BlockSpec((1, tq, LANES), lambda b, h, i: (b, i, h)),
            pl.BlockSpec((1, tq, LANES), lambda b, h, i: (b, i, nh + h // 2)),
            pl.BlockSpec((1, seq, LANES), lambda b, h, i: (b, 0, h)),
            pl.BlockSpec((1, seq, LANES), lambda b, h, i: (b, 0, kpe_block)),
            pl.BlockSpec((1, seq, LANES), lambda b, h, i: (b, 0, nh + h)),
            pl.BlockSpec((tq, LANES), lambda b, h, i: (i, 0)),
            pl.BlockSpec((tq, LANES), lambda b, h, i: (i, 0)),
            pl.BlockSpec((seq, LANES), lambda b, h, i: (0, 0)),
            pl.BlockSpec((seq, LANES), lambda b, h, i: (0, 0)),
        ],
        out_specs=pl.BlockSpec((1, tq, LANES), lambda b, h, i: (b, i, h)),
        out_shape=jax.ShapeDtypeStruct((bsz, seq, nh * MLA_V), F32),
        compiler_params=_params("parallel", "parallel", "parallel"),
        name="mla_attention",
    )(q_up, q_up, kv_up, h_all, kv_up, cos_t, sin_t, cos_t, sin_t)


def _band_attention(q, k, v, q0, k0, radius, scale):
    s = _dot_nt(q.astype(BF16), k.astype(BF16)) * scale
    qi = q0 + lax.broadcasted_iota(jnp.int32, s.shape, 0)
    kj = k0 + lax.broadcasted_iota(jnp.int32, s.shape, 1)
    s = jnp.where(jnp.abs(qi - kj) <= radius, s, -jnp.inf)
    m = jnp.max(s, axis=-1, keepdims=True)
    p = jnp.exp(s - m)
    den = jnp.sum(p, axis=-1, keepdims=True)
    o = jnp.dot(p.astype(BF16), v.astype(BF16), preferred_element_type=F32)
    return o / den, m + jnp.log(den)


def _dil_attn_kernel(q_ref, k_ref, v_ref, cos_ref, sin_ref, o_ref, lse_ref, kr_ref, *, seq, dilation, radius, scale):
    half = HEAD_DIM // 2
    length = seq // dilation

    def subsequence(r):
        rows = pl.ds(r, length, stride=dilation) if dilation > 1 else pl.ds(0, length)
        cos, sin = cos_ref[rows, :], sin_ref[rows, :]
        q = _rope(q_ref[0, rows, :], cos, sin, half)
        k = _rope(k_ref[0, rows, :], cos, sin, half)
        v = v_ref[0, rows, :]
        tq = min(length, 256)
        tk = min(length, tq + 4 * radius)
        if tk == length:
            o, lse = _band_attention(q, k, v, 0, 0, radius, scale)
            o_ref[0, rows, :] = o
            lse_ref[0, rows, :] = jnp.broadcast_to(lse, o.shape)
        else:
            kr_ref[...] = k
            for i in range(length // tq):
                k0 = min(max(i * tq - 2 * radius, 0), length - tk)
                o, lse = _band_attention(q[i * tq:(i + 1) * tq], kr_ref[k0:k0 + tk, :], v_ref[0, k0:k0 + tk, :],
                                         i * tq, k0, radius, scale)
                o_ref[0, i * tq:(i + 1) * tq, :] = o
                lse_ref[0, i * tq:(i + 1) * tq, :] = jnp.broadcast_to(lse, o.shape)

    if dilation == 1:
        subsequence(0)
    else:
        def body(r, carry):
            subsequence(r)
            return carry
        lax.fori_loop(0, dilation, body, 0)


def dilated_group_attention(h3, group, dilation, radius, cos_t, sin_t, seq):
    bsz = h3.shape[0]
    nh = DIL_HEADS
    base = group * 3 * nh
    out_sds = jax.ShapeDtypeStruct((bsz, seq, nh * HEAD_DIM), F32)
    blk = lambda off: pl.BlockSpec((1, seq, LANES), lambda b, h: (b, 0, base + off + h))
    tbl = pl.BlockSpec((seq, LANES), lambda b, h: (0, 0))
    out = pl.BlockSpec((1, seq, LANES), lambda b, h: (b, 0, h))
    o, lse = pl.pallas_call(
        functools.partial(_dil_attn_kernel, seq=seq, dilation=dilation, radius=radius, scale=HEAD_DIM ** -0.5),
        grid=(bsz, nh),
        in_specs=[blk(0), blk(nh), blk(2 * nh), tbl, tbl],
        out_specs=[out, out],
        out_shape=[out_sds, out_sds],
        scratch_shapes=[pltpu.VMEM((seq // dilation, LANES), F32)],
        compiler_params=_params("parallel", "parallel"),
        name=f"dilated_attention_g{group}",
    )(h3, h3, h3, cos_t, sin_t)
    return o.reshape(bsz * seq, nh * HEAD_DIM), lse.reshape(bsz * seq, nh * HEAD_DIM)


def _dil_combine_kernel(*refs):
    o_ref = refs[-1]
    n = (len(refs) - 1) // 2
    outs = [refs[2 * g][...] for g in range(n)]
    lses = [refs[2 * g + 1][...] for g in range(n)]
    m = functools.reduce(jnp.maximum, lses)
    es = [jnp.exp(l - m) for l in lses]
    num = functools.reduce(jnp.add, [e * o for e, o in zip(es, outs)])
    o_ref[...] = num / functools.reduce(jnp.add, es)


def dilated_combine(pairs, *, tm=256):
    m, d = pairs[0][0].shape
    flat = [a for pair in pairs for a in pair]
    spec = pl.BlockSpec((tm, d), lambda i: (i, 0))
    return pl.pallas_call(
        _dil_combine_kernel,
        grid=(m // tm,),
        in_specs=[spec] * len(flat),
        out_specs=spec,
        out_shape=jax.ShapeDtypeStruct((m, d), F32),
        compiler_params=_params("parallel"),
        name="dilated_combine",
    )(*flat)


def _top_rows(s, k, payload=None):
    rows = s.shape[0]
    idx = lax.broadcasted_iota(jnp.int32, s.shape, 0)
    vals, inds, pays = [], [], []
    for _ in range(k):
        m = jnp.max(s, axis=0, keepdims=True)
        sel = jnp.min(jnp.where(s == m, idx, rows), axis=0, keepdims=True)
        hit = idx == sel
        vals.append(m)
        inds.append(sel)
        if payload is not None:
            pays.append(jnp.sum(jnp.where(hit, payload, 0), axis=0, keepdims=True))
        s = jnp.where(hit, -jnp.inf, s)
    cat = lambda xs: jnp.concatenate(xs, axis=0)
    return cat(vals), cat(inds), (cat(pays) if payload is not None else None)


def _peer_select_kernel(qt_ref, keys_ref, e_ref, g_ref):
    kk = PEER_TOPK
    half = PEER_QUERY_DIM // 2
    tops = []
    for p in range(2):
        s = jnp.dot(keys_ref[p], qt_ref[p * half:(p + 1) * half, :], preferred_element_type=F32)
        tops.append(_top_rows(s, kk)[:2])
    (s0, i0), (s1, i1) = tops
    sub = lax.broadcasted_iota(jnp.int32, (8, s0.shape[1]), 0)
    pieces_s, pieces_e = [], []
    for a in range(kk // 2):
        nb = kk // (a + 1)
        rows = kk if nb > 8 else 8
        ps = s0[a:a + 1] + s1[0:rows]
        if nb < rows:
            ps = jnp.where(sub < nb, ps, -jnp.inf)
        pieces_s.append(ps)
        pieces_e.append(i0[a:a + 1] * PEER_N_KEYS + i1[0:rows])
    pieces_s.append(s0[kk // 2:kk] + s1[0:1])
    pieces_e.append(i0[kk // 2:kk] * PEER_N_KEYS + i1[0:1])
    cand_s = jnp.concatenate(pieces_s, axis=0)
    cand_e = jnp.concatenate(pieces_e, axis=0)
    best_s, _, experts = _top_rows(cand_s, kk, payload=cand_e)
    ex = jnp.exp(best_s - best_s[0:1])
    e_ref[0] = experts
    g_ref[0] = ex / jnp.sum(ex, axis=0, keepdims=True)


def peer_select(q_t, sub_keys, *, tt=128):
    t = q_t.shape[1]
    nh = PEER_HEADS
    kk = PEER_TOPK
    return pl.pallas_call(
        _peer_select_kernel,
        grid=(t // tt, nh),
        in_specs=[
            pl.BlockSpec((PEER_QUERY_DIM, tt), lambda i, h: (h, i)),
            pl.BlockSpec((2, PEER_N_KEYS, PEER_QUERY_DIM // 2), lambda i, h: (0, 0, 0)),
        ],
        out_specs=[
            pl.BlockSpec((1, kk, tt), lambda i, h: (h, 0, i)),
            pl.BlockSpec((1, kk, tt), lambda i, h: (h, 0, i)),
        ],
        out_shape=[jax.ShapeDtypeStruct((nh, kk, t), jnp.int32), jax.ShapeDtypeStruct((nh, kk, t), F32)],
        compiler_params=_params("parallel", "parallel"),
        name="peer_select",
    )(q_t, sub_keys)


def _erf(x):
    x = jnp.clip(x, -4.0, 4.0)
    x2 = x * x
    alpha = (-2.72614225801306e-10, 2.77068142495902e-08, -2.10102402082508e-06, -5.69250639462346e-05,
             -7.34990630326855e-04, -2.95459980854025e-03, -1.60960333262415e-02)
    beta = (-1.45660718464996e-05, -2.13374055278905e-04, -1.68282697438203e-03, -7.37332916720468e-03,
            -1.42647390514189e-02)
    p = jnp.full_like(x, alpha[0])
    for c in alpha[1:]:
        p = p * x2 + c
    q = jnp.full_like(x, beta[0])
    for c in beta[1:]:
        q = q * x2 + c
    return x * p / q


def _peer_expert_kernel(idx_hbm, g_ref, x_ref, uv_hbm, y_ref, idx_smem, buf, isem, sem, *, tb, d):
    n_sel = PEER_SEL
    step = pl.program_id(0)
    idx_copy = pltpu.make_async_copy(idx_hbm.at[pl.ds(step * (tb * n_sel), tb * n_sel)], idx_smem, isem)
    idx_copy.start()
    idx_copy.wait()

    def row_copy(slot, row, j):
        return pltpu.make_async_copy(uv_hbm.at[pl.ds(row, 1), :], buf.at[slot, pl.ds(j, 1), :], sem.at[slot])

    def issue(t, slot):
        for j in range(n_sel):
            row_copy(slot, idx_smem[t * n_sel + j], j).start(priority=j % 2)

    def wait(slot):
        for j in range(n_sel):
            row_copy(slot, 0, j).wait()

    ii = lax.broadcasted_iota(jnp.int32, (n_sel, n_sel), 0)
    jj = lax.broadcasted_iota(jnp.int32, (n_sel, n_sel), 1)
    eye = ii == jj

    issue(0, 0)

    def body(t, carry):
        slot = t % 2

        @pl.when(t + 1 < tb)
        def _():
            issue(t + 1, 1 - slot)

        wait(slot)
        x_row = x_ref[pl.ds(t, 1), :]
        hid = jnp.sum(buf[slot, :, 0:d] * x_row, axis=1, keepdims=True)
        g_col = jnp.sum(jnp.where(eye, g_ref[pl.ds(t, 1), :], 0.0), axis=1, keepdims=True)
        act = g_col * (0.5 * hid * (1.0 + _erf(hid * (2.0 ** -0.5))))
        y_ref[pl.ds(t, 1), :] = jnp.sum(buf[slot, :, d:2 * d] * act, axis=0, keepdims=True)
        return carry

    lax.fori_loop(0, tb, body, 0)


def peer_experts(idx, gates, x, uv, *, tb=64):
    t, d = x.shape
    n_sel = PEER_SEL
    return pl.pallas_call(
        functools.partial(_peer_expert_kernel, tb=tb, d=d),
        grid=(t // tb,),
        in_specs=[
            pl.BlockSpec(memory_space=pl.ANY),
            pl.BlockSpec((tb, n_sel), lambda i: (i, 0)),
            pl.BlockSpec((tb, d), lambda i: (i, 0)),
            pl.BlockSpec(memory_space=pl.ANY),
        ],
        out_specs=pl.BlockSpec((tb, d), lambda i: (i, 0)),
        out_shape=jax.ShapeDtypeStruct((t, d), F32),
        scratch_shapes=[
            pltpu.SMEM((tb * n_sel,), jnp.int32),
            pltpu.VMEM((2, n_sel, 2 * d), F32),
            pltpu.SemaphoreType.DMA,
            pltpu.SemaphoreType.DMA((2,)),
        ],
        compiler_params=_params("arbitrary"),
        name="peer_experts",
    )(idx, gates, x, uv)


def peer(x, w_q, sub_keys, u, v):
    t = x.shape[0]
    q_t = matmul_nt(w_q.T.astype(BF16), x)
    experts, gates = peer_select(q_t, sub_keys)
    idx = experts.reshape(PEER_SEL, t).T.reshape(t * PEER_SEL)
    gates = gates.reshape(PEER_SEL, t).T
    uv = jnp.concatenate([u, v], axis=1)
    return peer_experts(idx, gates, x, uv)


def _even_mixer(xt, bsz, seq, w_in, conv_w, gate_params, norm_g, q_norm_g, w_uq, kv_norm_g, w_ukv, w_out,
                cos_m, sin_m):
    nh = DN_HEADS
    c_qkvz = 4 * nh * DN_DK
    c_gate = c_qkvz + 4 * nh
    c_cq = c_gate + MLA_Q_RANK
    c_ckv = c_cq + MLA_KV_RANK
    w_gate = w_in[:, c_qkvz:c_gate]
    w_kr = w_in[:, c_ckv:]
    w_all = jnp.concatenate(
        [w_in[:, :c_qkvz], w_in[:, c_gate:c_ckv], w_kr, w_kr, w_gate,
         jnp.zeros((w_in.shape[0], LANES - 4 * nh), w_in.dtype)], axis=1).astype(BF16)
    h_all = matmul(xt, w_all, tn=768)
    cols = w_all.shape[1]
    h3d = h_all.reshape(bsz, seq, cols)
    lat0 = c_qkvz
    kpe_block = (lat0 + MLA_Q_RANK + MLA_KV_RANK) // LANES
    g0 = lat0 + MLA_Q_RANK + MLA_KV_RANK + LANES

    qkv = deltanet_conv(h3d, conv_w, seq)
    n_chunks = seq // DN_CHUNK
    gates = h_all[:, g0:g0 + 4 * nh].reshape(bsz, n_chunks, DN_CHUNK, 4, nh).transpose(0, 4, 3, 1, 2)
    o2 = deltanet_scan(qkv, gates, gate_params, seq)
    o_a = deltanet_gated_norm(o2, h3d, 3 * nh, norm_g, seq)

    hq = MLA_NOPE + MLA_ROPE
    w_uq_r = w_uq.reshape(MLA_Q_RANK, MLA_HEADS, hq)
    w_uq_p = jnp.concatenate([w_uq_r[:, :, :MLA_NOPE].reshape(MLA_Q_RANK, -1),
                              w_uq_r[:, :, MLA_NOPE:].reshape(MLA_Q_RANK, -1)], axis=1).astype(BF16)
    w_ukv_r = w_ukv.reshape(MLA_KV_RANK, MLA_HEADS, MLA_NOPE + MLA_V)
    w_ukv_p = jnp.concatenate([w_ukv_r[:, :, :MLA_NOPE].reshape(MLA_KV_RANK, -1),
                               w_ukv_r[:, :, MLA_NOPE:].reshape(MLA_KV_RANK, -1)], axis=1).astype(BF16)
    q_up = rms_matmul(h_all, lat0 // MLA_Q_RANK, q_norm_g, w_uq_p)
    kv_up = rms_matmul(h_all, (lat0 + MLA_Q_RANK) // MLA_KV_RANK, kv_norm_g, w_ukv_p)
    o_b = mla_attention(q_up.reshape(bsz, seq, -1), kv_up.reshape(bsz, seq, -1), h3d, kpe_block, cos_m, sin_m, seq)

    n_a = nh * DN_DV
    w_out = w_out.astype(BF16)
    return matmul2(o_a.reshape(bsz * seq, n_a), w_out[:n_a], o_b.reshape(bsz * seq, -1), w_out[n_a:])


def _odd_mixer(xt, bsz, seq, w_in, w_out, cos_h, sin_h):
    h3 = matmul(xt, w_in.astype(BF16), tn=768).reshape(bsz, seq, -1)
    pairs = []
    for g, (window, dilation) in enumerate(DIL_GROUPS):
        pairs.append(dilated_group_attention(h3, g, dilation, window // (2 * dilation), cos_h, sin_h, seq))
    o = dilated_combine(pairs)
    return matmul(o, w_out.astype(BF16))


def kernel(x, ab_w_in, a_conv_w, a_log_f, a_dt_bias_f, a_log_b, a_dt_bias_b, a_out_norm_g, b_q_norm_g, b_w_uq,
           b_kv_norm_g, b_w_ukv, ab_w_out, c_w_in, c_w_out, mix_ln_g, mix_ln_b, peer_w_q, peer_sub_keys, peer_u,
           peer_v, ffn_ln_g, ffn_ln_b):
    bsz, seq, d = x.shape
    depth = mix_ln_g.shape[0]
    alpha = (2 * depth) ** 0.25
    cos_h, sin_h = _rope_tables(seq, HEAD_DIM)
    cos_m, sin_m = _rope_tables(seq, MLA_ROPE)
    xt = x.reshape(bsz * seq, d)
    for layer in range(depth):
        i = layer // 2
        if layer % 2 == 0:
            gate_params = jnp.stack([a_log_f[i], a_dt_bias_f[i], a_log_b[i], a_dt_bias_b[i]], axis=1)
            mix = _even_mixer(xt, bsz, seq, ab_w_in[i], a_conv_w[i], gate_params, a_out_norm_g[i], b_q_norm_g[i],
                              b_w_uq[i], b_kv_norm_g[i], b_w_ukv[i], ab_w_out[i], cos_m, sin_m)
        else:
            mix = _odd_mixer(xt, bsz, seq, c_w_in[i], c_w_out[i], cos_h, sin_h)
        xt = add_layer_norm(xt, mix, mix_ln_g[layer], mix_ln_b[layer], alpha)
        ffn = peer(xt, peer_w_q[layer], peer_sub_keys[layer], peer_u[layer], peer_v[layer])
        xt = add_layer_norm(xt, ffn, ffn_ln_g[layer], ffn_ln_b[layer], alpha)
    return xt.reshape(bsz, seq, d)
```

```python
import functools

import jax
import jax.numpy as jnp
from jax import lax
from jax.experimental import pallas as pl
from jax.experimental.pallas import tpu as pltpu

F32 = jnp.float32
BF16 = jnp.bfloat16
HIGHEST = lax.Precision.HIGHEST

HEAD_DIM = 128
ROPE_THETA = 10000.0
EPS = 1e-5
LANES = 128

DN_HEADS = 8
DN_DK = 128
DN_DV = 128
DN_CONV = 5
DN_CHUNK = 64

MLA_HEADS = 8
MLA_Q_RANK = 512
MLA_KV_RANK = 512
MLA_NOPE = 128
MLA_ROPE = 64
MLA_V = 128

DIL_GROUPS = ((128, 1), (512, 4), (2048, 16))
DIL_HEADS = 8

PEER_HEADS = 8
PEER_N_KEYS = 128
PEER_QUERY_DIM = 256
PEER_TOPK = 16
PEER_SEL = PEER_HEADS * PEER_TOPK

VMEM_LIMIT = 48 * 1024 * 1024


def _params(*sem):
    return pltpu.CompilerParams(dimension_semantics=sem, vmem_limit_bytes=VMEM_LIMIT)


def _dot_nt(a, b, **kw):
    return lax.dot_general(a, b, (((1,), (1,)), ((), ())), preferred_element_type=F32, **kw)


def _mm_kernel(a_ref, b_ref, o_ref):
    o_ref[...] = jnp.dot(a_ref[...].astype(BF16), b_ref[...], preferred_element_type=F32).astype(o_ref.dtype)


def matmul(a, b, *, tm=512, tn=512, out_dtype=F32):
    m, k = a.shape
    n = b.shape[1]
    return pl.pallas_call(
        _mm_kernel,
        grid=(n // tn, m // tm),
        in_specs=[pl.BlockSpec((tm, k), lambda j, i: (i, 0)), pl.BlockSpec((k, tn), lambda j, i: (0, j))],
        out_specs=pl.BlockSpec((tm, tn), lambda j, i: (i, j)),
        out_shape=jax.ShapeDtypeStruct((m, n), out_dtype),
        compiler_params=_params("parallel", "parallel"),
        name="matmul",
    )(a, b)


def _mm2_kernel(a1_ref, b1_ref, a2_ref, b2_ref, o_ref):
    acc = jnp.dot(a1_ref[...].astype(BF16), b1_ref[...], preferred_element_type=F32)
    acc += jnp.dot(a2_ref[...].astype(BF16), b2_ref[...], preferred_element_type=F32)
    o_ref[...] = acc


def matmul2(a1, b1, a2, b2, *, tm=512, tn=512):
    m, k1 = a1.shape
    k2 = a2.shape[1]
    n = b1.shape[1]
    return pl.pallas_call(
        _mm2_kernel,
        grid=(n // tn, m // tm),
        in_specs=[
            pl.BlockSpec((tm, k1), lambda j, i: (i, 0)),
            pl.BlockSpec((k1, tn), lambda j, i: (0, j)),
            pl.BlockSpec((tm, k2), lambda j, i: (i, 0)),
            pl.BlockSpec((k2, tn), lambda j, i: (0, j)),
        ],
        out_specs=pl.BlockSpec((tm, tn), lambda j, i: (i, j)),
        out_shape=jax.ShapeDtypeStruct((m, n), F32),
        compiler_params=_params("parallel", "parallel"),
        name="matmul2",
    )(a1, b1, a2, b2)


def _mm_nt_kernel(a_ref, b_ref, o_ref):
    o_ref[...] = _dot_nt(a_ref[...], b_ref[...].astype(BF16))


def matmul_nt(a, b, *, tm=512, tn=512):
    m, k = a.shape
    n = b.shape[0]
    return pl.pallas_call(
        _mm_nt_kernel,
        grid=(m // tm, n // tn),
        in_specs=[pl.BlockSpec((tm, k), lambda i, j: (i, 0)), pl.BlockSpec((tn, k), lambda i, j: (j, 0))],
        out_specs=pl.BlockSpec((tm, tn), lambda i, j: (i, j)),
        out_shape=jax.ShapeDtypeStruct((m, n), F32),
        compiler_params=_params("parallel", "parallel"),
        name="matmul_nt",
    )(a, b)


def _rms_mm_kernel(a_ref, g_ref, b_ref, o_ref):
    a = a_ref[...]
    a = a * lax.rsqrt(jnp.mean(a * a, axis=-1, keepdims=True) + EPS) * g_ref[...]
    o_ref[...] = jnp.dot(a.astype(BF16), b_ref[...], preferred_element_type=F32)


def rms_matmul(a, col_block, g, b, *, tm=512, tn=512):
    m = a.shape[0]
    k, n = b.shape
    return pl.pallas_call(
        _rms_mm_kernel,
        grid=(n // tn, m // tm),
        in_specs=[
            pl.BlockSpec((tm, k), lambda j, i: (i, col_block)),
            pl.BlockSpec((1, k), lambda j, i: (0, 0)),
            pl.BlockSpec((k, tn), lambda j, i: (0, j)),
        ],
        out_specs=pl.BlockSpec((tm, tn), lambda j, i: (i, j)),
        out_shape=jax.ShapeDtypeStruct((m, n), F32),
        compiler_params=_params("parallel", "parallel"),
        name="rms_matmul",
    )(a, g.reshape(1, k), b)


def _add_ln_kernel(x_ref, y_ref, g_ref, b_ref, o_ref, *, alpha):
    z = alpha * x_ref[...] + y_ref[...]
    mu = jnp.mean(z, axis=-1, keepdims=True)
    zc = z - mu
    var = jnp.mean(zc * zc, axis=-1, keepdims=True)
    o_ref[...] = zc * lax.rsqrt(var + EPS) * g_ref[...] + b_ref[...]


def add_layer_norm(x, y, g, b, alpha, *, tm=256):
    m, d = x.shape
    return pl.pallas_call(
        functools.partial(_add_ln_kernel, alpha=alpha),
        grid=(m // tm,),
        in_specs=[
            pl.BlockSpec((tm, d), lambda i: (i, 0)),
            pl.BlockSpec((tm, d), lambda i: (i, 0)),
            pl.BlockSpec((1, d), lambda i: (0, 0)),
            pl.BlockSpec((1, d), lambda i: (0, 0)),
        ],
        out_specs=pl.BlockSpec((tm, d), lambda i: (i, 0)),
        out_shape=jax.ShapeDtypeStruct((m, d), F32),
        compiler_params=_params("parallel"),
        name="add_layer_norm",
    )(x, y, g.reshape(1, d), b.reshape(1, d))


def _rope_tables(seq, dim):
    half = dim // 2
    inv_freq = ROPE_THETA ** (-jnp.arange(0, dim, 2, dtype=F32) / dim)
    ang = jnp.arange(seq, dtype=F32)[:, None] * inv_freq[None, :]
    cos, sin = jnp.cos(ang), jnp.sin(ang)
    reps = LANES // dim
    cos_t = jnp.tile(jnp.concatenate([cos, cos], axis=-1), (1, reps))
    sin_t = jnp.tile(jnp.concatenate([-sin, sin], axis=-1), (1, reps))
    return cos_t, sin_t


def _rope(x, cos_t, sin_t, half):
    if 2 * half == LANES:
        partner = pltpu.roll(x, half, 1)
    else:
        lane = lax.broadcasted_iota(jnp.int32, x.shape, 1)
        first = (lane % (2 * half)) < half
        partner = jnp.where(first, pltpu.roll(x, LANES - half, 1), pltpu.roll(x, half, 1))
    return x * cos_t + partner * sin_t


def _conv_kernel(x_ref, w_ref, o_ref, pad_ref, *, seq, width, n_norm, n_scaled, scale):
    c = pl.program_id(1)
    half = (width - 1) // 2
    pad = 8
    pad_ref[0:pad, :] = jnp.zeros((pad, LANES), F32)
    pad_ref[pad + seq:pad + seq + pad, :] = jnp.zeros((pad, LANES), F32)
    pad_ref[pad:pad + seq, :] = x_ref[0]
    acc = jnp.zeros((seq, LANES), F32)
    for j in range(width):
        acc += w_ref[j:j + 1, :] * pad_ref[pl.ds(pad + j - half, seq), :]
    y = acc * jax.nn.sigmoid(acc)
    nrm = y * lax.rsqrt(jnp.sum(y * y, axis=-1, keepdims=True) + 1e-6)
    nrm = nrm * jnp.where(c < n_scaled, scale, 1.0)
    o_ref[0] = jnp.where(c < n_norm, nrm, y)


def deltanet_conv(h_all, conv_w, seq):
    bsz = h_all.shape[0]
    n_ch = conv_w.shape[1]
    width = conv_w.shape[0]
    return pl.pallas_call(
        functools.partial(_conv_kernel, seq=seq, width=width, n_norm=2 * DN_HEADS, n_scaled=DN_HEADS,
                          scale=DN_DK ** -0.5),
        grid=(bsz, n_ch // LANES),
        in_specs=[
            pl.BlockSpec((1, seq, LANES), lambda b, c: (b, 0, c)),
            pl.BlockSpec((width, LANES), lambda b, c: (0, c)),
        ],
        out_specs=pl.BlockSpec((1, seq, LANES), lambda b, c: (b, 0, c)),
        out_shape=jax.ShapeDtypeStruct((bsz, seq, n_ch), F32),
        scratch_shapes=[pltpu.VMEM((seq + 16, LANES), F32)],
        compiler_params=_params("parallel", "parallel"),
        name="deltanet_conv",
    )(h_all, conv_w)


def _delta_chunks(chains):
    hdot = functools.partial(jnp.dot, preferred_element_type=F32, precision=HIGHEST)
    dot = functools.partial(jnp.dot, preferred_element_type=F32)
    chunk = chains[0]["qc"].shape[0]
    for c in chains:
        g_row = c["neg_a"] * jax.nn.softplus(c["a_row"] + c["dt_bias"])
        beta_row = jax.nn.sigmoid(c["b_row"])
        gc_col = jnp.sum(jnp.where(c["lower"], g_row, 0.0), axis=1, keepdims=True)
        gc_row = jnp.sum(jnp.where(c["eye"], gc_col, 0.0), axis=0, keepdims=True)
        beta_col = jnp.sum(jnp.where(c["eye"], beta_row, 0.0), axis=1, keepdims=True)
        c["decay"] = jnp.where(c["lower"], jnp.exp(jnp.minimum(gc_col - gc_row, 0.0)), 0.0)
        c["kb"] = c["kc"] * beta_col
        c["vb"] = c["vc"] * beta_col
        c["egc"] = jnp.exp(gc_col)
        c["g_last"] = jnp.sum(g_row, axis=1, keepdims=True)
        c["k_dec"] = c["kc"] * jnp.exp(c["g_last"] - gc_col)
    for c in chains:
        c["kk"] = _dot_nt(c["kb"], c["kc"])
    for c in chains:
        c["qk"] = _dot_nt(c["qc"], c["kc"])
    for c in chains:
        c["xp"] = -jnp.where(c["strict"], c["kk"] * c["decay"], 0.0)
        c["tinv"] = c["eye"].astype(F32) + c["xp"]
    span = 2
    while span < chunk:
        for c in chains:
            c["xp"] = hdot(c["xp"], c["xp"])
        for c in chains:
            c["tinv"] = c["tinv"] + hdot(c["tinv"], c["xp"])
        span *= 2
    for c in chains:
        c["uw"] = hdot(c["tinv"], jnp.concatenate([c["vb"], c["kb"] * c["egc"]], axis=1))
    for c in chains:
        dv = c["vb"].shape[1]
        c["v_new"] = c["uw"][:, :dv] - dot(c["uw"][:, dv:], c["state"])
    for c in chains:
        c["out"] = dot(c["qc"] * c["egc"], c["state"])
    for c in chains:
        c["out"] = c["out"] + dot(c["qk"] * c["decay"], c["v_new"])
    for c in chains:
        c["new_state"] = c["state"] * jnp.exp(c["g_last"]) + dot(c["k_dec"].T, c["v_new"])
    return [(c["out"], c["new_state"]) for c in chains]


def _delta_kernel(p_ref, q_ref, k_ref, v_ref, gt_ref, o_ref, s_ref, *, seq, chunk, heads_per_step):
    hg = pl.program_id(1)
    n_chunks = seq // chunk
    ii = lax.broadcasted_iota(jnp.int32, (chunk, chunk), 0)
    jj = lax.broadcasted_iota(jnp.int32, (chunk, chunk), 1)
    eye = ii == jj
    masks = ((ii >= jj, ii > jj), (ii <= jj, ii < jj))
    neg_a, dt_bias = [], []
    for hh in range(heads_per_step):
        for d in range(2):
            h = hg * heads_per_step + hh
            neg_a.append(-jnp.exp(jnp.full((1, chunk), p_ref[h, 2 * d], F32)))
            dt_bias.append(p_ref[h, 2 * d + 1])
    s_ref[...] = jnp.zeros_like(s_ref)

    def body(n, carry):
        chains, where = [], []
        for hh in range(heads_per_step):
            cols = slice(hh * LANES, (hh + 1) * LANES)
            for d in range(2):
                ch = 2 * hh + d
                c = n if d == 0 else n_chunks - 1 - n
                r0 = pl.multiple_of(c * chunk, chunk)
                rows = pl.ds(r0, chunk)
                chains.append(dict(
                    qc=q_ref[0, rows, cols], kc=k_ref[0, rows, cols], vc=v_ref[0, rows, cols],
                    a_row=gt_ref[0, hh, 2 * d, pl.ds(c, 1), :], b_row=gt_ref[0, hh, 2 * d + 1, pl.ds(c, 1), :],
                    neg_a=neg_a[ch], dt_bias=dt_bias[ch], lower=masks[d][0], strict=masks[d][1], eye=eye,
                    state=s_ref[ch]))
                where.append((ch, d, rows, cols))
        for (ch, d, rows, cols), (out, new_state) in zip(where, _delta_chunks(chains)):
            s_ref[ch] = new_state
            o_ref[d, 0, rows, cols] = out
        return carry

    lax.fori_loop(0, n_chunks, body, 0)


def deltanet_scan(qkv, gates, gate_params, seq, *, heads_per_step=4):
    bsz = qkv.shape[0]
    nh = DN_HEADS
    chunk = DN_CHUNK
    n_chunks = seq // chunk
    hps = heads_per_step
    groups = nh // hps
    width = hps * LANES
    return pl.pallas_call(
        functools.partial(_delta_kernel, seq=seq, chunk=chunk, heads_per_step=hps),
        grid=(bsz, groups),
        in_specs=[
            pl.BlockSpec(memory_space=pltpu.SMEM),
            pl.BlockSpec((1, seq, width), lambda b, g: (b, 0, g)),
            pl.BlockSpec((1, seq, width), lambda b, g: (b, 0, groups + g)),
            pl.BlockSpec((1, seq, width), lambda b, g: (b, 0, 2 * groups + g)),
            pl.BlockSpec((1, hps, 4, n_chunks, chunk), lambda b, g: (b, g, 0, 0, 0)),
        ],
        out_specs=pl.BlockSpec((2, 1, seq, width), lambda b, g: (0, b, 0, g)),
        out_shape=jax.ShapeDtypeStruct((2, bsz, seq, nh * DN_DV), F32),
        scratch_shapes=[pltpu.VMEM((2 * hps, DN_DK, DN_DV), F32)],
        compiler_params=_params("parallel", "parallel"),
        name="deltanet_scan",
    )(gate_params, qkv, qkv, qkv, gates)


def _gated_norm_kernel(of_ref, ob_ref, z_ref, g_ref, o_ref):
    o = of_ref[0, 0] + ob_ref[0, 0]
    o = o * lax.rsqrt(jnp.mean(o * o, axis=-1, keepdims=True) + EPS) * g_ref[...]
    z = z_ref[0]
    o_ref[0] = o * (z * jax.nn.sigmoid(z))


def deltanet_gated_norm(o2, h_all, z_block0, norm_g, seq, *, ts=512):
    bsz = o2.shape[1]
    nh = DN_HEADS
    return pl.pallas_call(
        _gated_norm_kernel,
        grid=(bsz, seq // ts, nh),
        in_specs=[
            pl.BlockSpec((1, 1, ts, LANES), lambda b, i, h: (0, b, i, h)),
            pl.BlockSpec((1, 1, ts, LANES), lambda b, i, h: (1, b, i, h)),
            pl.BlockSpec((1, ts, LANES), lambda b, i, h: (b, i, z_block0 + h)),
            pl.BlockSpec((1, LANES), lambda b, i, h: (0, 0)),
        ],
        out_specs=pl.BlockSpec((1, ts, LANES), lambda b, i, h: (b, i, h)),
        out_shape=jax.ShapeDtypeStruct((bsz, seq, nh * DN_DV), F32),
        compiler_params=_params("parallel", "parallel", "parallel"),
        name="deltanet_gated_norm",
    )(o2, o2, h_all, norm_g.reshape(1, DN_DV))


def _mla_attn_kernel(qn_ref, qpe_ref, kn_ref, kpe_ref, v_ref, cq_ref, sq_ref, ck_ref, sk_ref, o_ref, *, scale):
    h = pl.program_id(1)
    half = MLA_ROPE // 2
    qpe = _rope(qpe_ref[0], cq_ref[...], sq_ref[...], half)
    lane = lax.broadcasted_iota(jnp.int32, qpe.shape, 1)
    qpe = jnp.where((lane // MLA_ROPE) == (h % 2), qpe, 0.0)
    kpe = _rope(kpe_ref[0], ck_ref[...], sk_ref[...], half)
    s = _dot_nt(qn_ref[0].astype(BF16), kn_ref[0].astype(BF16))
    s += _dot_nt(qpe.astype(BF16), kpe.astype(BF16))
    s = s * scale
    m = jnp.max(s, axis=-1, keepdims=True)
    p = jnp.exp(s - m)
    den = jnp.sum(p, axis=-1, keepdims=True)
    o = jnp.dot(p.astype(BF16), v_ref[0].astype(BF16), preferred_element_type=F32)
    o_ref[0] = o / den


def mla_attention(q_up, kv_up, h_all, kpe_block, cos_t, sin_t, seq, *, tq=512):
    bsz = q_up.shape[0]
    nh = MLA_HEADS
    scale = (MLA_NOPE + MLA_ROPE) ** -0.5
    return pl.pallas_call(
        functools.partial(_mla_attn_kernel, scale=scale),
        grid=(bsz, nh, seq // tq),
        in_specs=[
            pl.BlockSpec((1, tq, LANES), lambda b, h, i: (b, i, h)),
            pl.BlockSpec((1, tq, LANES), lambda b, h, i: (b, i, nh + h // 2)),
            pl.BlockSpec((1, seq, LANES), lambda b, h, i: (b, 0, h)),
            pl.BlockSpec((1, seq, LANES), lambda b, h, i: (b, 0, kpe_block)),
            pl.BlockSpec((1, seq, LANES), lambda b, h, i: (b, 0, nh + h)),
            pl.BlockSpec((tq, LANES), lambda b, h, i: (i, 0)),
            pl.BlockSpec((tq, LANES), lambda b, h, i: (i, 0)),
            pl.BlockSpec((seq, LANES), lambda b, h, i: (0, 0)),
            pl.BlockSpec((seq, LANES), lambda b, h, i: (0, 0)),
        ],
        out_specs=pl.BlockSpec((1, tq, LANES), lambda b, h, i: (b, i, h)),
        out_shape=jax.ShapeDtypeStruct((bsz, seq, nh * MLA_V), F32),
        compiler_params=_params("parallel", "parallel", "parallel"),
        name="mla_attention",
    )(q_up, q_up, kv_up, h_all, kv_up, cos_t, sin_t, cos_t, sin_t)


def _band_attention(q, k, v, q0, k0, radius, scale):
    s = _dot_nt(q.astype(BF16), k.astype(BF16)) * scale
    qi = q0 + lax.broadcasted_iota(jnp.int32, s.shape, 0)
    kj = k0 + lax.broadcasted_iota(jnp.int32, s.shape, 1)
    s = jnp.where(jnp.abs(qi - kj) <= radius, s, -jnp.inf)
    m = jnp.max(s, axis=-1, keepdims=True)
    p = jnp.exp(s - m)
    den = jnp.sum(p, axis=-1, keepdims=True)
    o = jnp.dot(p.astype(BF16), v.astype(BF16), preferred_element_type=F32)
    return o / den, m + jnp.log(den)


def _dil_attn_kernel(q_ref, k_ref, v_ref, cos_ref, sin_ref, o_ref, lse_ref, kr_ref, *, seq, dilation, radius, scale):
    half = HEAD_DIM // 2
    length = seq // dilation

    def subsequence(r):
        rows = pl.ds(r, length, stride=dilation) if dilation > 1 else pl.ds(0, length)
        cos, sin = cos_ref[rows, :], sin_ref[rows, :]
        q = _rope(q_ref[0, rows, :], cos, sin, half)
        k = _rope(k_ref[0, rows, :], cos, sin, half)
        v = v_ref[0, rows, :]
        tq = min(length, 256)
        tk = min(length, tq + 4 * radius)
        if tk == length:
            o, lse = _band_attention(q, k, v, 0, 0, radius, scale)
            o_ref[0, rows, :] = o
            lse_ref[0, rows, :] = jnp.broadcast_to(lse, o.shape)
        else:
            kr_ref[...] = k
            for i in range(length // tq):
                k0 = min(max(i * tq - 2 * radius, 0), length - tk)
                o, lse = _band_attention(q[i * tq:(i + 1) * tq], kr_ref[k0:k0 + tk, :], v_ref[0, k0:k0 + tk, :],
                                         i * tq, k0, radius, scale)
                o_ref[0, i * tq:(i + 1) * tq, :] = o
                lse_ref[0, i * tq:(i + 1) * tq, :] = jnp.broadcast_to(lse, o.shape)

    if dilation == 1:
        subsequence(0)
    else:
        def body(r, carry):
            subsequence(r)
            return carry
        lax.fori_loop(0, dilation, body, 0)


def dilated_group_attention(h3, group, dilation, radius, cos_t, sin_t, seq):
    bsz = h3.shape[0]
    nh = DIL_HEADS
    base = group * 3 * nh
    out_sds = jax.ShapeDtypeStruct((bsz, seq, nh * HEAD_DIM), F32)
    blk = lambda off: pl.BlockSpec((1, seq, LANES), lambda b, h: (b, 0, base + off + h))
    tbl = pl.BlockSpec((seq, LANES), lambda b, h: (0, 0))
    out = pl.BlockSpec((1, seq, LANES), lambda b, h: (b, 0, h))
    o, lse = pl.pallas_call(
        functools.partial(_dil_attn_kernel, seq=seq, dilation=dilation, radius=radius, scale=HEAD_DIM ** -0.5),
        grid=(bsz, nh),
        in_specs=[blk(0), blk(nh), blk(2 * nh), tbl, tbl],
        out_specs=[out, out],
        out_shape=[out_sds, out_sds],
        scratch_shapes=[pltpu.VMEM((seq // dilation, LANES), F32)],
        compiler_params=_params("parallel", "parallel"),
        name=f"dilated_attention_g{group}",
    )(h3, h3, h3, cos_t, sin_t)
    return o.reshape(bsz * seq, nh * HEAD_DIM), lse.reshape(bsz * seq, nh * HEAD_DIM)


def _dil_combine_kernel(*refs):
    o_ref = refs[-1]
    n = (len(refs) - 1) // 2
    outs = [refs[2 * g][...] for g in range(n)]
    lses = [refs[2 * g + 1][...] for g in range(n)]
    m = functools.reduce(jnp.maximum, lses)
    es = [jnp.exp(l - m) for l in lses]
    num = functools.reduce(jnp.add, [e * o for e, o in zip(es, outs)])
    o_ref[...] = num / functools.reduce(jnp.add, es)


def dilated_combine(pairs, *, tm=256):
    m, d = pairs[0][0].shape
    flat = [a for pair in pairs for a in pair]
    spec = pl.BlockSpec((tm, d), lambda i: (i, 0))
    return pl.pallas_call(
        _dil_combine_kernel,
        grid=(m // tm,),
        in_specs=[spec] * len(flat),
        out_specs=spec,
        out_shape=jax.ShapeDtypeStruct((m, d), F32),
        compiler_params=_params("parallel"),
        name="dilated_combine",
    )(*flat)


def _top_rows(s, k, payload=None):
    rows = s.shape[0]
    idx = lax.broadcasted_iota(jnp.int32, s.shape, 0)
    vals, inds, pays = [], [], []
    for _ in range(k):
        m = jnp.max(s, axis=0, keepdims=True)
        sel = jnp.min(jnp.where(s == m, idx, rows), axis=0, keepdims=True)
        hit = idx == sel
        vals.append(m)
        inds.append(sel)
        if payload is not None:
            pays.append(jnp.sum(jnp.where(hit, payload, 0), axis=0, keepdims=True))
        s = jnp.where(hit, -jnp.inf, s)
    cat = lambda xs: jnp.concatenate(xs, axis=0)
    return cat(vals), cat(inds), (cat(pays) if payload is not None else None)


def _peer_select_kernel(qt_ref, keys_ref, e_ref, g_ref):
    kk = PEER_TOPK
    half = PEER_QUERY_DIM // 2
    tops = []
    for p in range(2):
        s = jnp.dot(keys_ref[p], qt_ref[p * half:(p + 1) * half, :], preferred_element_type=F32)
        tops.append(_top_rows(s, kk)[:2])
    (s0, i0), (s1, i1) = tops
    sub = lax.broadcasted_iota(jnp.int32, (8, s0.shape[1]), 0)
    pieces_s, pieces_e = [], []
    for a in range(kk // 2):
        nb = kk // (a + 1)
        rows = kk if nb > 8 else 8
        ps = s0[a:a + 1] + s1[0:rows]
        if nb < rows:
            ps = jnp.where(sub < nb, ps, -jnp.inf)
        pieces_s.append(ps)
        pieces_e.append(i0[a:a + 1] * PEER_N_KEYS + i1[0:rows])
    pieces_s.append(s0[kk // 2:kk] + s1[0:1])
    pieces_e.append(i0[kk // 2:kk] * PEER_N_KEYS + i1[0:1])
    cand_s = jnp.concatenate(pieces_s, axis=0)
    cand_e = jnp.concatenate(pieces_e, axis=0)
    best_s, _, experts = _top_rows(cand_s, kk, payload=cand_e)
    ex = jnp.exp(best_s - best_s[0:1])
    e_ref[0] = experts
    g_ref[0] = ex / jnp.sum(ex, axis=0, keepdims=True)


def peer_select(q_t, sub_keys, *, tt=128):
    t = q_t.shape[1]
    nh = PEER_HEADS
    kk = PEER_TOPK
    return pl.pallas_call(
        _peer_select_kernel,
        grid=(t // tt, nh),
        in_specs=[
            pl.BlockSpec((PEER_QUERY_DIM, tt), lambda i, h: (h, i)),
            pl.BlockSpec((2, PEER_N_KEYS, PEER_QUERY_DIM // 2), lambda i, h: (0, 0, 0)),
        ],
        out_specs=[
            pl.BlockSpec((1, kk, tt), lambda i, h: (h, 0, i)),
            pl.BlockSpec((1, kk, tt), lambda i, h: (h, 0, i)),
        ],
        out_shape=[jax.ShapeDtypeStruct((nh, kk, t), jnp.int32), jax.ShapeDtypeStruct((nh, kk, t), F32)],
        compiler_params=_params("parallel", "parallel"),
        name="peer_select",
    )(q_t, sub_keys)


def _erf(x):
    x = jnp.clip(x, -4.0, 4.0)
    x2 = x * x
    alpha = (-2.72614225801306e-10, 2.77068142495902e-08, -2.10102402082508e-06, -5.69250639462346e-05,
             -7.34990630326855e-04, -2.95459980854025e-03, -1.60960333262415e-02)
    beta = (-1.45660718464996e-05, -2.13374055278905e-04, -1.68282697438203e-03, -7.37332916720468e-03,
            -1.42647390514189e-02)
    p = jnp.full_like(x, alpha[0])
    for c in alpha[1:]:
        p = p * x2 + c
    q = jnp.full_like(x, beta[0])
    for c in beta[1:]:
        q = q * x2 + c
    return x * p / q


PEER_SLOTS = 3


def _peer_expert_kernel(idx_hbm, g_ref, x_ref, uv_hbm, lng_ref, lnb_ref, y_ref, idx_smem, buf, isem, sem, *, tb, d, alpha):
    n_sel = PEER_SEL
    n_chunks = d // LANES
    step = pl.program_id(0)
    idx_copy = pltpu.make_async_copy(idx_hbm.at[pl.ds(step * (tb * n_sel), tb * n_sel)], idx_smem, isem)
    idx_copy.start()
    idx_copy.wait()

    def row_copy(slot, row, j):
        return pltpu.make_async_copy(uv_hbm.at[row], buf.at[slot, :, j, :], sem.at[slot])

    def issue(t, slot):
        for j in range(n_sel):
            row_copy(slot, idx_smem[t * n_sel + j], j).start(priority=j % 2)

    def wait(slot):
        for j in range(n_sel):
            row_copy(slot, 0, j).wait()

    ii = lax.broadcasted_iota(jnp.int32, (n_sel, n_sel), 0)
    jj = lax.broadcasted_iota(jnp.int32, (n_sel, n_sel), 1)
    eye = ii == jj
    high = jnp.uint32(0xFFFF0000)

    for t0 in range(PEER_SLOTS - 1):
        issue(t0, t0)

    def body(t, carry):
        slot = t % PEER_SLOTS
        ahead = t + (PEER_SLOTS - 1)

        @pl.when(ahead < tb)
        def _():
            issue(ahead, ahead % PEER_SLOTS)

        wait(slot)
        x_row = x_ref[pl.ds(t, 1), :]
        acc = jnp.zeros((n_sel, LANES), F32)
        for c in range(n_chunks):
            u_c = lax.bitcast_convert_type(buf[slot, c] & high, F32)
            acc += u_c * x_row[:, c * LANES:(c + 1) * LANES]
        hid = jnp.sum(acc, axis=1, keepdims=True)
        g_col = jnp.sum(jnp.where(eye, g_ref[pl.ds(t, 1), :], 0.0), axis=1, keepdims=True)
        act = g_col * (0.5 * hid * (1.0 + _erf(hid * (2.0 ** -0.5))))
        y_chunks = []
        for c in range(n_chunks):
            v_c = lax.bitcast_convert_type(buf[slot, c] << 16, F32)
            y_chunks.append(jnp.sum(v_c * act, axis=0, keepdims=True))
        y_ref[pl.ds(t, 1), :] = jnp.concatenate(y_chunks, axis=1)
        return carry

    lax.fori_loop(0, tb, body, 0)
    z = alpha * x_ref[...] + y_ref[...]
    mu = jnp.mean(z, axis=-1, keepdims=True)
    zc = z - mu
    var = jnp.mean(zc * zc, axis=-1, keepdims=True)
    y_ref[...] = zc * lax.rsqrt(var + EPS) * lng_ref[...] + lnb_ref[...]


def peer_experts(idx, gates, x, uv, ln_g, ln_b, alpha, *, tb=128):
    t, d = x.shape
    n_sel = PEER_SEL
    return pl.pallas_call(
        functools.partial(_peer_expert_kernel, tb=tb, d=d, alpha=alpha),
        grid=(t // tb,),
        in_specs=[
            pl.BlockSpec(memory_space=pl.ANY),
            pl.BlockSpec((tb, n_sel), lambda i: (i, 0)),
            pl.BlockSpec((tb, d), lambda i: (i, 0)),
            pl.BlockSpec(memory_space=pl.ANY),
            pl.BlockSpec((1, d), lambda i: (0, 0)),
            pl.BlockSpec((1, d), lambda i: (0, 0)),
        ],
        out_specs=pl.BlockSpec((tb, d), lambda i: (i, 0)),
        out_shape=jax.ShapeDtypeStruct((t, d), F32),
        scratch_shapes=[
            pltpu.SMEM((tb * n_sel,), jnp.int32),
            pltpu.VMEM((PEER_SLOTS, d // LANES, n_sel, LANES), jnp.uint32),
            pltpu.SemaphoreType.DMA,
            pltpu.SemaphoreType.DMA((PEER_SLOTS,)),
        ],
        compiler_params=_params("arbitrary"),
        name="peer_experts",
    )(idx, gates, x, uv, ln_g.reshape(1, d), ln_b.reshape(1, d))


def _pack_expert_tables(u, v):
    e, d = u.shape
    hi = lax.bitcast_convert_type(u.astype(BF16), jnp.uint16).astype(jnp.uint32) << 16
    lo = lax.bitcast_convert_type(v.astype(BF16), jnp.uint16).astype(jnp.uint32)
    return (hi | lo).reshape(e, d // LANES, LANES)


def peer_layer(x, w_q, sub_keys, u, v, ln_g, ln_b, alpha):
    t = x.shape[0]
    q_t = matmul_nt(w_q.T.astype(BF16), x)
    experts, gates = peer_select(q_t, sub_keys)
    idx = experts.reshape(PEER_SEL, t).T.reshape(t * PEER_SEL)
    gates = gates.reshape(PEER_SEL, t).T
    return peer_experts(idx, gates, x, _pack_expert_tables(u, v), ln_g, ln_b, alpha)


def _even_mixer(xt, bsz, seq, w_in, conv_w, gate_params, norm_g, q_norm_g, w_uq, kv_norm_g, w_ukv, w_out,
                cos_m, sin_m):
    nh = DN_HEADS
    c_qkvz = 4 * nh * DN_DK
    c_gate = c_qkvz + 4 * nh
    c_cq = c_gate + MLA_Q_RANK
    c_ckv = c_cq + MLA_KV_RANK
    w_gate = w_in[:, c_qkvz:c_gate]
    w_kr = w_in[:, c_ckv:]
    w_all = jnp.concatenate(
        [w_in[:, :c_qkvz], w_in[:, c_gate:c_ckv], w_kr, w_kr, w_gate,
         jnp.zeros((w_in.shape[0], LANES - 4 * nh), w_in.dtype)], axis=1).astype(BF16)
    h_all = matmul(xt, w_all, tn=768)
    cols = w_all.shape[1]
    h3d = h_all.reshape(bsz, seq, cols)
    lat0 = c_qkvz
    kpe_block = (lat0 + MLA_Q_RANK + MLA_KV_RANK) // LANES
    g0 = lat0 + MLA_Q_RANK + MLA_KV_RANK + LANES

    qkv = deltanet_conv(h3d, conv_w, seq)
    n_chunks = seq // DN_CHUNK
    gates = h_all[:, g0:g0 + 4 * nh].reshape(bsz, n_chunks, DN_CHUNK, 4, nh).transpose(0, 4, 3, 1, 2)
    o2 = deltanet_scan(qkv, gates, gate_params, seq)
    o_a = deltanet_gated_norm(o2, h3d, 3 * nh, norm_g, seq)

    hq = MLA_NOPE + MLA_ROPE
    w_uq_r = w_uq.reshape(MLA_Q_RANK, MLA_HEADS, hq)
    w_uq_p = jnp.concatenate([w_uq_r[:, :, :MLA_NOPE].reshape(MLA_Q_RANK, -1),
                              w_uq_r[:, :, MLA_NOPE:].reshape(MLA_Q_RANK, -1)], axis=1).astype(BF16)
    w_ukv_r = w_ukv.reshape(MLA_KV_RANK, MLA_HEADS, MLA_NOPE + MLA_V)
    w_ukv_p = jnp.concatenate([w_ukv_r[:, :, :MLA_NOPE].reshape(MLA_KV_RANK, -1),
                               w_ukv_r[:, :, MLA_NOPE:].reshape(MLA_KV_RANK, -1)], axis=1).astype(BF16)
    q_up = rms_matmul(h_all, lat0 // MLA_Q_RANK, q_norm_g, w_uq_p)
    kv_up = rms_matmul(h_all, (lat0 + MLA_Q_RANK) // MLA_KV_RANK, kv_norm_g, w_ukv_p)
    o_b = mla_attention(q_up.reshape(bsz, seq, -1), kv_up.reshape(bsz, seq, -1), h3d, kpe_block, cos_m, sin_m, seq)

    n_a = nh * DN_DV
    w_out = w_out.astype(BF16)
    return matmul2(o_a.reshape(bsz * seq, n_a), w_out[:n_a], o_b.reshape(bsz * seq, -1), w_out[n_a:])


def _odd_mixer(xt, bsz, seq, w_in, w_out, cos_h, sin_h):
    h3 = matmul(xt, w_in.astype(BF16), tn=768).reshape(bsz, seq, -1)
    pairs = []
    for g, (window, dilation) in enumerate(DIL_GROUPS):
        pairs.append(dilated_group_attention(h3, g, dilation, window // (2 * dilation), cos_h, sin_h, seq))
    o = dilated_combine(pairs)
    return matmul(o, w_out.astype(BF16))


def kernel(x, ab_w_in, a_conv_w, a_log_f, a_dt_bias_f, a_log_b, a_dt_bias_b, a_out_norm_g, b_q_norm_g, b_w_uq,
           b_kv_norm_g, b_w_ukv, ab_w_out, c_w_in, c_w_out, mix_ln_g, mix_ln_b, peer_w_q, peer_sub_keys, peer_u,
           peer_v, ffn_ln_g, ffn_ln_b):
    bsz, seq, d = x.shape
    depth = mix_ln_g.shape[0]
    alpha = (2 * depth) ** 0.25
    cos_h, sin_h = _rope_tables(seq, HEAD_DIM)
    cos_m, sin_m = _rope_tables(seq, MLA_ROPE)
    xt = x.reshape(bsz * seq, d)
    for layer in range(depth):
        i = layer // 2
        if layer % 2 == 0:
            gate_params = jnp.stack([a_log_f[i], a_dt_bias_f[i], a_log_b[i], a_dt_bias_b[i]], axis=1)
            mix = _even_mixer(xt, bsz, seq, ab_w_in[i], a_conv_w[i], gate_params, a_out_norm_g[i], b_q_norm_g[i],
                              b_w_uq[i], b_kv_norm_g[i], b_w_ukv[i], ab_w_out[i], cos_m, sin_m)
        else:
            mix = _odd_mixer(xt, bsz, seq, c_w_in[i], c_w_out[i], cos_h, sin_h)
        xt = add_layer_norm(xt, mix, mix_ln_g[layer], mix_ln_b[layer], alpha)
        xt = peer_layer(xt, peer_w_q[layer], peer_sub_keys[layer], peer_u[layer], peer_v[layer],
                        ffn_ln_g[layer], ffn_ln_b[layer], alpha)
    return xt.reshape(bsz, seq, d)
```

```python
import functools

import jax
import jax.numpy as jnp
from jax import lax
from jax.experimental import pallas as pl
from jax.experimental.pallas import tpu as pltpu

F32 = jnp.float32
BF16 = jnp.bfloat16
HIGHEST = lax.Precision.HIGHEST

HEAD_DIM = 128
ROPE_THETA = 10000.0
EPS = 1e-5
LANES = 128

DN_HEADS = 8
DN_DK = 128
DN_DV = 128
DN_CONV = 5
DN_CHUNK = 64

MLA_HEADS = 8
MLA_Q_RANK = 512
MLA_KV_RANK = 512
MLA_NOPE = 128
MLA_ROPE = 64
MLA_V = 128

DIL_GROUPS = ((128, 1), (512, 4), (2048, 16))
DIL_HEADS = 8

PEER_HEADS = 8
PEER_N_KEYS = 128
PEER_QUERY_DIM = 256
PEER_TOPK = 16
PEER_SEL = PEER_HEADS * PEER_TOPK

VMEM_LIMIT = 48 * 1024 * 1024


def _params(*sem):
    return pltpu.CompilerParams(dimension_semantics=sem, vmem_limit_bytes=VMEM_LIMIT)


def _dot_nt(a, b, **kw):
    return lax.dot_general(a, b, (((1,), (1,)), ((), ())), preferred_element_type=F32, **kw)


def _mm_kernel(a_ref, b_ref, o_ref):
    o_ref[...] = jnp.dot(a_ref[...].astype(BF16), b_ref[...], preferred_element_type=F32).astype(o_ref.dtype)


def matmul(a, b, *, tm=512, tn=512, out_dtype=F32):
    m, k = a.shape
    n = b.shape[1]
    return pl.pallas_call(
        _mm_kernel,
        grid=(n // tn, m // tm),
        in_specs=[pl.BlockSpec((tm, k), lambda j, i: (i, 0)), pl.BlockSpec((k, tn), lambda j, i: (0, j))],
        out_specs=pl.BlockSpec((tm, tn), lambda j, i: (i, j)),
        out_shape=jax.ShapeDtypeStruct((m, n), out_dtype),
        compiler_params=_params("parallel", "parallel"),
        name="matmul",
    )(a, b)


def _mm2_kernel(a1_ref, b1_ref, a2_ref, b2_ref, o_ref):
    acc = jnp.dot(a1_ref[...].astype(BF16), b1_ref[...], preferred_element_type=F32)
    acc += jnp.dot(a2_ref[...].astype(BF16), b2_ref[...], preferred_element_type=F32)
    o_ref[...] = acc


def matmul2(a1, b1, a2, b2, *, tm=512, tn=512):
    m, k1 = a1.shape
    k2 = a2.shape[1]
    n = b1.shape[1]
    return pl.pallas_call(
        _mm2_kernel,
        grid=(n // tn, m // tm),
        in_specs=[
            pl.BlockSpec((tm, k1), lambda j, i: (i, 0)),
            pl.BlockSpec((k1, tn), lambda j, i: (0, j)),
            pl.BlockSpec((tm, k2), lambda j, i: (i, 0)),
            pl.BlockSpec((k2, tn), lambda j, i: (0, j)),
        ],
        out_specs=pl.BlockSpec((tm, tn), lambda j, i: (i, j)),
        out_shape=jax.ShapeDtypeStruct((m, n), F32),
        compiler_params=_params("parallel", "parallel"),
        name="matmul2",
    )(a1, b1, a2, b2)


def _mm_nt_kernel(a_ref, b_ref, o_ref):
    o_ref[...] = _dot_nt(a_ref[...], b_ref[...].astype(BF16))


def matmul_nt(a, b, *, tm=512, tn=512):
    m, k = a.shape
    n = b.shape[0]
    return pl.pallas_call(
        _mm_nt_kernel,
        grid=(m // tm, n // tn),
        in_specs=[pl.BlockSpec((tm, k), lambda i, j: (i, 0)), pl.BlockSpec((tn, k), lambda i, j: (j, 0))],
        out_specs=pl.BlockSpec((tm, tn), lambda i, j: (i, j)),
        out_shape=jax.ShapeDtypeStruct((m, n), F32),
        compiler_params=_params("parallel", "parallel"),
        name="matmul_nt",
    )(a, b)


def _rms_mm_kernel(a_ref, g_ref, b_ref, o_ref):
    a = a_ref[...]
    a = a * lax.rsqrt(jnp.mean(a * a, axis=-1, keepdims=True) + EPS) * g_ref[...]
    o_ref[...] = jnp.dot(a.astype(BF16), b_ref[...], preferred_element_type=F32)


def rms_matmul(a, col_block, g, b, *, tm=512, tn=512):
    m = a.shape[0]
    k, n = b.shape
    return pl.pallas_call(
        _rms_mm_kernel,
        grid=(n // tn, m // tm),
        in_specs=[
            pl.BlockSpec((tm, k), lambda j, i: (i, col_block)),
            pl.BlockSpec((1, k), lambda j, i: (0, 0)),
            pl.BlockSpec((k, tn), lambda j, i: (0, j)),
        ],
        out_specs=pl.BlockSpec((tm, tn), lambda j, i: (i, j)),
        out_shape=jax.ShapeDtypeStruct((m, n), F32),
        compiler_params=_params("parallel", "parallel"),
        name="rms_matmul",
    )(a, g.reshape(1, k), b)


def _add_ln_kernel(x_ref, y_ref, g_ref, b_ref, o_ref, *, alpha):
    z = alpha * x_ref[...] + y_ref[...]
    mu = jnp.mean(z, axis=-1, keepdims=True)
    zc = z - mu
    var = jnp.mean(zc * zc, axis=-1, keepdims=True)
    o_ref[...] = zc * lax.rsqrt(var + EPS) * g_ref[...] + b_ref[...]


def add_layer_norm(x, y, g, b, alpha, *, tm=256):
    m, d = x.shape
    return pl.pallas_call(
        functools.partial(_add_ln_kernel, alpha=alpha),
        grid=(m // tm,),
        in_specs=[
            pl.BlockSpec((tm, d), lambda i: (i, 0)),
            pl.BlockSpec((tm, d), lambda i: (i, 0)),
            pl.BlockSpec((1, d), lambda i: (0, 0)),
            pl.BlockSpec((1, d), lambda i: (0, 0)),
        ],
        out_specs=pl.BlockSpec((tm, d), lambda i: (i, 0)),
        out_shape=jax.ShapeDtypeStruct((m, d), F32),
        compiler_params=_params("parallel"),
        name="add_layer_norm",
    )(x, y, g.reshape(1, d), b.reshape(1, d))


def _rope_tables(seq, dim):
    half = dim // 2
    inv_freq = ROPE_THETA ** (-jnp.arange(0, dim, 2, dtype=F32) / dim)
    ang = jnp.arange(seq, dtype=F32)[:, None] * inv_freq[None, :]
    cos, sin = jnp.cos(ang), jnp.sin(ang)
    reps = LANES // dim
    cos_t = jnp.tile(jnp.concatenate([cos, cos], axis=-1), (1, reps))
    sin_t = jnp.tile(jnp.concatenate([-sin, sin], axis=-1), (1, reps))
    return cos_t, sin_t


def _rope(x, cos_t, sin_t, half):
    if 2 * half == LANES:
        partner = pltpu.roll(x, half, 1)
    else:
        lane = lax.broadcasted_iota(jnp.int32, x.shape, 1)
        first = (lane % (2 * half)) < half
        partner = jnp.where(first, pltpu.roll(x, LANES - half, 1), pltpu.roll(x, half, 1))
    return x * cos_t + partner * sin_t


def _conv_kernel(x_ref, w_ref, o_ref, pad_ref, *, seq, width, n_norm, n_scaled, scale):
    c = pl.program_id(1)
    half = (width - 1) // 2
    pad = 8
    pad_ref[0:pad, :] = jnp.zeros((pad, LANES), F32)
    pad_ref[pad + seq:pad + seq + pad, :] = jnp.zeros((pad, LANES), F32)
    pad_ref[pad:pad + seq, :] = x_ref[0]
    acc = jnp.zeros((seq, LANES), F32)
    for j in range(width):
        acc += w_ref[j:j + 1, :] * pad_ref[pl.ds(pad + j - half, seq), :]
    y = acc * jax.nn.sigmoid(acc)
    nrm = y * lax.rsqrt(jnp.sum(y * y, axis=-1, keepdims=True) + 1e-6)
    nrm = nrm * jnp.where(c < n_scaled, scale, 1.0)
    o_ref[0] = jnp.where(c < n_norm, nrm, y)


def deltanet_conv(h_all, conv_w, seq):
    bsz = h_all.shape[0]
    n_ch = conv_w.shape[1]
    width = conv_w.shape[0]
    return pl.pallas_call(
        functools.partial(_conv_kernel, seq=seq, width=width, n_norm=2 * DN_HEADS, n_scaled=DN_HEADS,
                          scale=DN_DK ** -0.5),
        grid=(bsz, n_ch // LANES),
        in_specs=[
            pl.BlockSpec((1, seq, LANES), lambda b, c: (b, 0, c)),
            pl.BlockSpec((width, LANES), lambda b, c: (0, c)),
        ],
        out_specs=pl.BlockSpec((1, seq, LANES), lambda b, c: (b, 0, c)),
        out_shape=jax.ShapeDtypeStruct((bsz, seq, n_ch), F32),
        scratch_shapes=[pltpu.VMEM((seq + 16, LANES), F32)],
        compiler_params=_params("parallel", "parallel"),
        name="deltanet_conv",
    )(h_all, conv_w)


def _delta_chunks(chains):
    hdot = functools.partial(jnp.dot, preferred_element_type=F32, precision=HIGHEST)
    dot = functools.partial(jnp.dot, preferred_element_type=F32)
    chunk = chains[0]["qc"].shape[0]
    for c in chains:
        g_row = c["neg_a"] * jax.nn.softplus(c["a_row"] + c["dt_bias"])
        beta_row = jax.nn.sigmoid(c["b_row"])
        gc_col = jnp.sum(jnp.where(c["lower"], g_row, 0.0), axis=1, keepdims=True)
        gc_row = jnp.sum(jnp.where(c["eye"], gc_col, 0.0), axis=0, keepdims=True)
        beta_col = jnp.sum(jnp.where(c["eye"], beta_row, 0.0), axis=1, keepdims=True)
        c["decay"] = jnp.where(c["lower"], jnp.exp(jnp.minimum(gc_col - gc_row, 0.0)), 0.0)
        c["kb"] = c["kc"] * beta_col
        c["vb"] = c["vc"] * beta_col
        c["egc"] = jnp.exp(gc_col)
        c["g_last"] = jnp.sum(g_row, axis=1, keepdims=True)
        c["k_dec"] = c["kc"] * jnp.exp(c["g_last"] - gc_col)
    for c in chains:
        c["kk"] = _dot_nt(c["kb"], c["kc"])
    for c in chains:
        c["qk"] = _dot_nt(c["qc"], c["kc"])
    for c in chains:
        c["xp"] = -jnp.where(c["strict"], c["kk"] * c["decay"], 0.0)
        c["tinv"] = c["eye"].astype(F32) + c["xp"]
    span = 2
    while span < chunk:
        for c in chains:
            c["xp"] = hdot(c["xp"], c["xp"])
        for c in chains:
            c["tinv"] = c["tinv"] + hdot(c["tinv"], c["xp"])
        span *= 2
    for c in chains:
        c["uw"] = hdot(c["tinv"], jnp.concatenate([c["vb"], c["kb"] * c["egc"]], axis=1))
    for c in chains:
        dv = c["vb"].shape[1]
        c["v_new"] = c["uw"][:, :dv] - dot(c["uw"][:, dv:], c["state"])
    for c in chains:
        c["out"] = dot(c["qc"] * c["egc"], c["state"])
    for c in chains:
        c["out"] = c["out"] + dot(c["qk"] * c["decay"], c["v_new"])
    for c in chains:
        c["new_state"] = c["state"] * jnp.exp(c["g_last"]) + dot(c["k_dec"].T, c["v_new"])
    return [(c["out"], c["new_state"]) for c in chains]


def _delta_kernel(p_ref, q_ref, k_ref, v_ref, gt_ref, o_ref, s_ref, *, seq, chunk, heads_per_step):
    hg = pl.program_id(1)
    n_chunks = seq // chunk
    ii = lax.broadcasted_iota(jnp.int32, (chunk, chunk), 0)
    jj = lax.broadcasted_iota(jnp.int32, (chunk, chunk), 1)
    eye = ii == jj
    masks = ((ii >= jj, ii > jj), (ii <= jj, ii < jj))
    neg_a, dt_bias = [], []
    for hh in range(heads_per_step):
        for d in range(2):
            h = hg * heads_per_step + hh
            neg_a.append(-jnp.exp(jnp.full((1, chunk), p_ref[h, 2 * d], F32)))
            dt_bias.append(p_ref[h, 2 * d + 1])
    s_ref[...] = jnp.zeros_like(s_ref)

    def body(n, carry):
        chains, where = [], []
        for hh in range(heads_per_step):
            cols = slice(hh * LANES, (hh + 1) * LANES)
            for d in range(2):
                ch = 2 * hh + d
                c = n if d == 0 else n_chunks - 1 - n
                r0 = pl.multiple_of(c * chunk, chunk)
                rows = pl.ds(r0, chunk)
                chains.append(dict(
                    qc=q_ref[0, rows, cols], kc=k_ref[0, rows, cols], vc=v_ref[0, rows, cols],
                    a_row=gt_ref[0, hh, 2 * d, pl.ds(c, 1), :], b_row=gt_ref[0, hh, 2 * d + 1, pl.ds(c, 1), :],
                    neg_a=neg_a[ch], dt_bias=dt_bias[ch], lower=masks[d][0], strict=masks[d][1], eye=eye,
                    state=s_ref[ch]))
                where.append((ch, d, rows, cols))
        for (ch, d, rows, cols), (out, new_state) in zip(where, _delta_chunks(chains)):
            s_ref[ch] = new_state
            o_ref[d, 0, rows, cols] = out
        return carry

    lax.fori_loop(0, n_chunks, body, 0)


def deltanet_scan(qkv, gates, gate_params, seq, *, heads_per_step=4):
    bsz = qkv.shape[0]
    nh = DN_HEADS
    chunk = DN_CHUNK
    n_chunks = seq // chunk
    hps = heads_per_step
    groups = nh // hps
    width = hps * LANES
    return pl.pallas_call(
        functools.partial(_delta_kernel, seq=seq, chunk=chunk, heads_per_step=hps),
        grid=(bsz, groups),
        in_specs=[
            pl.BlockSpec(memory_space=pltpu.SMEM),
            pl.BlockSpec((1, seq, width), lambda b, g: (b, 0, g)),
            pl.BlockSpec((1, seq, width), lambda b, g: (b, 0, groups + g)),
            pl.BlockSpec((1, seq, width), lambda b, g: (b, 0, 2 * groups + g)),
            pl.BlockSpec((1, hps, 4, n_chunks, chunk), lambda b, g: (b, g, 0, 0, 0)),
        ],
        out_specs=pl.BlockSpec((2, 1, seq, width), lambda b, g: (0, b, 0, g)),
        out_shape=jax.ShapeDtypeStruct((2, bsz, seq, nh * DN_DV), F32),
        scratch_shapes=[pltpu.VMEM((2 * hps, DN_DK, DN_DV), F32)],
        compiler_params=_params("parallel", "parallel"),
        name="deltanet_scan",
    )(gate_params, qkv, qkv, qkv, gates)


def _gated_norm_kernel(of_ref, ob_ref, z_ref, g_ref, o_ref):
    o = of_ref[0, 0] + ob_ref[0, 0]
    o = o * lax.rsqrt(jnp.mean(o * o, axis=-1, keepdims=True) + EPS) * g_ref[...]
    z = z_ref[0]
    o_ref[0] = o * (z * jax.nn.sigmoid(z))


def deltanet_gated_norm(o2, h_all, z_block0, norm_g, seq, *, ts=512):
    bsz = o2.shape[1]
    nh = DN_HEADS
    return pl.pallas_call(
        _gated_norm_kernel,
        grid=(bsz, seq // ts, nh),
        in_specs=[
            pl.BlockSpec((1, 1, ts, LANES), lambda b, i, h: (0, b, i, h)),
            pl.BlockSpec((1, 1, ts, LANES), lambda b, i, h: (1, b, i, h)),
            pl.BlockSpec((1, ts, LANES), lambda b, i, h: (b, i, z_block0 + h)),
            pl.BlockSpec((1, LANES), lambda b, i, h: (0, 0)),
        ],
        out_specs=pl.BlockSpec((1, ts, LANES), lambda b, i, h: (b, i, h)),
        out_shape=jax.ShapeDtypeStruct((bsz, seq, nh * DN_DV), F32),
        compiler_params=_params("parallel", "parallel", "parallel"),
        name="deltanet_gated_norm",
    )(o2, o2, h_all, norm_g.reshape(1, DN_DV))


def _mla_attn_kernel(qn_ref, qpe_ref, kn_ref, kpe_ref, v_ref, cq_ref, sq_ref, ck_ref, sk_ref, o_ref, *, scale):
    h = pl.program_id(1)
    half = MLA_ROPE // 2
    qpe = _rope(qpe_ref[0], cq_ref[...], sq_ref[...], half)
    lane = lax.broadcasted_iota(jnp.int32, qpe.shape, 1)
    qpe = jnp.where((lane // MLA_ROPE) == (h % 2), qpe, 0.0)
    kpe = _rope(kpe_ref[0], ck_ref[...], sk_ref[...], half)
    s = _dot_nt(qn_ref[0].astype(BF16), kn_ref[0].astype(BF16))
    s += _dot_nt(qpe.astype(BF16), kpe.astype(BF16))
    s = s * scale
    m = jnp.max(s, axis=-1, keepdims=True)
    p = jnp.exp(s - m)
    den = jnp.sum(p, axis=-1, keepdims=True)
    o = jnp.dot(p.astype(BF16), v_ref[0].astype(BF16), preferred_element_type=F32)
    o_ref[0] = o / den


def mla_attention(q_up, kv_up, h_all, kpe_block, cos_t, sin_t, seq, *, tq=512):
    bsz = q_up.shape[0]
    nh = MLA_HEADS
    scale = (MLA_NOPE + MLA_ROPE) ** -0.5
    return pl.pallas_call(
        functools.partial(_mla_attn_kernel, scale=scale),
        grid=(bsz, nh, seq // tq),
        in_specs=[
            pl.BlockSpec((1, tq, LANES), lambda b, h, i: (b, i, h)),
            pl.BlockSpec((1, tq, LANES), lambda b, h, i: (b, i, nh + h // 2)),
            pl.BlockSpec((1, seq, LANES), lambda b, h, i: (b, 0, h)),
            pl.BlockSpec((1, seq, LANES), lambda b, h, i: (b, 0, kpe_block)),
            pl.BlockSpec((1, seq, LANES), lambda b, h, i: (b, 0, nh + h)),
            pl.BlockSpec((tq, LANES), lambda b, h, i: (i, 0)),
            pl.BlockSpec((tq, LANES), lambda b, h, i: (i, 0)),
            pl.BlockSpec((seq, LANES), lambda b, h, i: (0, 0)),
            pl.BlockSpec((seq, LANES), lambda b, h, i: (0, 0)),
        ],
        out_specs=pl.BlockSpec((1, tq, LANES), lambda b, h, i: (b, i, h)),
        out_shape=jax.ShapeDtypeStruct((bsz, seq, nh * MLA_V), F32),
        compiler_params=_params("parallel", "parallel", "parallel"),
        name="mla_attention",
    )(q_up, q_up, kv_up, h_all, kv_up, cos_t, sin_t, cos_t, sin_t)


def _band_attention(q, k, v, q0, k0, radius, scale):
    s = _dot_nt(q.astype(BF16), k.astype(BF16)) * scale
    qi = q0 + lax.broadcasted_iota(jnp.int32, s.shape, 0)
    kj = k0 + lax.broadcasted_iota(jnp.int32, s.shape, 1)
    s = jnp.where(jnp.abs(qi - kj) <= radius, s, -jnp.inf)
    m = jnp.max(s, axis=-1, keepdims=True)
    p = jnp.exp(s - m)
    den = jnp.sum(p, axis=-1, keepdims=True)
    o = jnp.dot(p.astype(BF16), v.astype(BF16), preferred_element_type=F32)
    return o / den, m + jnp.log(den)


def _dil_attn_kernel(q_ref, k_ref, v_ref, cos_ref, sin_ref, o_ref, lse_ref, kr_ref, *, seq, dilation, radius, scale):
    half = HEAD_DIM // 2
    length = seq // dilation

    def subsequence(r):
        rows = pl.ds(r, length, stride=dilation) if dilation > 1 else pl.ds(0, length)
        cos, sin = cos_ref[rows, :], sin_ref[rows, :]
        q = _rope(q_ref[0, rows, :], cos, sin, half)
        k = _rope(k_ref[0, rows, :], cos, sin, half)
        v = v_ref[0, rows, :]
        tq = min(length, 256)
        tk = min(length, tq + 4 * radius)
        if tk == length:
            o, lse = _band_attention(q, k, v, 0, 0, radius, scale)
            o_ref[0, rows, :] = o
            lse_ref[0, rows, :] = jnp.broadcast_to(lse, o.shape)
        else:
            kr_ref[...] = k
            for i in range(length // tq):
                k0 = min(max(i * tq - 2 * radius, 0), length - tk)
                o, lse = _band_attention(q[i * tq:(i + 1) * tq], kr_ref[k0:k0 + tk, :], v_ref[0, k0:k0 + tk, :],
                                         i * tq, k0, radius, scale)
                o_ref[0, i * tq:(i + 1) * tq, :] = o
                lse_ref[0, i * tq:(i + 1) * tq, :] = jnp.broadcast_to(lse, o.shape)

    if dilation == 1:
        subsequence(0)
    else:
        def body(r, carry):
            subsequence(r)
            return carry
        lax.fori_loop(0, dilation, body, 0)


def dilated_group_attention(h3, group, dilation, radius, cos_t, sin_t, seq):
    bsz = h3.shape[0]
    nh = DIL_HEADS
    base = group * 3 * nh
    out_sds = jax.ShapeDtypeStruct((bsz, seq, nh * HEAD_DIM), F32)
    blk = lambda off: pl.BlockSpec((1, seq, LANES), lambda b, h: (b, 0, base + off + h))
    tbl = pl.BlockSpec((seq, LANES), lambda b, h: (0, 0))
    out = pl.BlockSpec((1, seq, LANES), lambda b, h: (b, 0, h))
    o, lse = pl.pallas_call(
        functools.partial(_dil_attn_kernel, seq=seq, dilation=dilation, radius=radius, scale=HEAD_DIM ** -0.5),
        grid=(bsz, nh),
        in_specs=[blk(0), blk(nh), blk(2 * nh), tbl, tbl],
        out_specs=[out, out],
        out_shape=[out_sds, out_sds],
        scratch_shapes=[pltpu.VMEM((seq // dilation, LANES), F32)],
        compiler_params=_params("parallel", "parallel"),
        name=f"dilated_attention_g{group}",
    )(h3, h3, h3, cos_t, sin_t)
    return o.reshape(bsz * seq, nh * HEAD_DIM), lse.reshape(bsz * seq, nh * HEAD_DIM)


def _dil_combine_kernel(*refs):
    o_ref = refs[-1]
    n = (len(refs) - 1) // 2
    outs = [refs[2 * g][...] for g in range(n)]
    lses = [refs[2 * g + 1][...] for g in range(n)]
    m = functools.reduce(jnp.maximum, lses)
    es = [jnp.exp(l - m) for l in lses]
    num = functools.reduce(jnp.add, [e * o for e, o in zip(es, outs)])
    o_ref[...] = num / functools.reduce(jnp.add, es)


def dilated_combine(pairs, *, tm=256):
    m, d = pairs[0][0].shape
    flat = [a for pair in pairs for a in pair]
    spec = pl.BlockSpec((tm, d), lambda i: (i, 0))
    return pl.pallas_call(
        _dil_combine_kernel,
        grid=(m // tm,),
        in_specs=[spec] * len(flat),
        out_specs=spec,
        out_shape=jax.ShapeDtypeStruct((m, d), F32),
        compiler_params=_params("parallel"),
        name="dilated_combine",
    )(*flat)


def _top_rows(s, k, payload=None):
    rows = s.shape[0]
    idx = lax.broadcasted_iota(jnp.int32, s.shape, 0).astype(F32)
    vals, inds, pays = [], [], []
    for _ in range(k):
        m = jnp.max(s, axis=0, keepdims=True)
        sel = jnp.min(jnp.where(s == m, idx, float(rows)), axis=0, keepdims=True)
        hit = idx == sel
        vals.append(m)
        inds.append(sel)
        if payload is not None:
            pays.append(jnp.sum(jnp.where(hit, payload, 0.0), axis=0, keepdims=True))
        s = jnp.where(hit, -jnp.inf, s)
    cat = lambda xs: jnp.concatenate(xs, axis=0)
    return cat(vals), cat(inds), (cat(pays) if payload is not None else None)


def _peer_select_kernel(qt_ref, keys_ref, e_ref, g_ref):
    kk = PEER_TOPK
    half = PEER_QUERY_DIM // 2
    for lb in range(qt_ref.shape[1] // LANES):
        lanes = slice(lb * LANES, (lb + 1) * LANES)
        tops = []
        for p in range(2):
            s = jnp.dot(keys_ref[p], qt_ref[p * half:(p + 1) * half, lanes], preferred_element_type=F32)
            tops.append(_top_rows(s, kk)[:2])
        (s0, i0), (s1, i1) = tops
        sub = lax.broadcasted_iota(jnp.int32, (8, LANES), 0)
        pieces_s, pieces_e = [], []
        for a in range(kk // 2):
            nb = kk // (a + 1)
            rows = kk if nb > 8 else 8
            ps = s0[a:a + 1] + s1[0:rows]
            if nb < rows:
                ps = jnp.where(sub < nb, ps, -jnp.inf)
            pieces_s.append(ps)
            pieces_e.append(i0[a:a + 1] * float(PEER_N_KEYS) + i1[0:rows])
        pieces_s.append(s0[kk // 2:kk] + s1[0:1])
        pieces_e.append(i0[kk // 2:kk] * float(PEER_N_KEYS) + i1[0:1])
        cand_s = jnp.concatenate(pieces_s, axis=0)
        cand_e = jnp.concatenate(pieces_e, axis=0)
        best_s, _, experts = _top_rows(cand_s, kk, payload=cand_e)
        ex = jnp.exp(best_s - best_s[0:1])
        e_ref[0, :, lanes] = experts.astype(jnp.int32)
        g_ref[0, :, lanes] = ex / jnp.sum(ex, axis=0, keepdims=True)


def peer_select(q_t, sub_keys, *, tt=512):
    t = q_t.shape[1]
    nh = PEER_HEADS
    kk = PEER_TOPK
    return pl.pallas_call(
        _peer_select_kernel,
        grid=(t // tt, nh),
        in_specs=[
            pl.BlockSpec((PEER_QUERY_DIM, tt), lambda i, h: (h, i)),
            pl.BlockSpec((2, PEER_N_KEYS, PEER_QUERY_DIM // 2), lambda i, h: (0, 0, 0)),
        ],
        out_specs=[
            pl.BlockSpec((1, kk, tt), lambda i, h: (h, 0, i)),
            pl.BlockSpec((1, kk, tt), lambda i, h: (h, 0, i)),
        ],
        out_shape=[jax.ShapeDtypeStruct((nh, kk, t), jnp.int32), jax.ShapeDtypeStruct((nh, kk, t), F32)],
        compiler_params=_params("parallel", "parallel"),
        name="peer_select",
    )(q_t, sub_keys)


def _erf(x):
    x = jnp.clip(x, -4.0, 4.0)
    x2 = x * x
    alpha = (-2.72614225801306e-10, 2.77068142495902e-08, -2.10102402082508e-06, -5.69250639462346e-05,
             -7.34990630326855e-04, -2.95459980854025e-03, -1.60960333262415e-02)
    beta = (-1.45660718464996e-05, -2.13374055278905e-04, -1.68282697438203e-03, -7.37332916720468e-03,
            -1.42647390514189e-02)
    p = jnp.full_like(x, alpha[0])
    for c in alpha[1:]:
        p = p * x2 + c
    q = jnp.full_like(x, beta[0])
    for c in beta[1:]:
        q = q * x2 + c
    return x * p / q


PEER_SLOTS = 4


def _peer_expert_kernel(idx_hbm, g_ref, x_ref, uv_hbm, lng_ref, lnb_ref, y_ref, idx_smem, buf, isem, sem, *, tb, d, alpha):
    n_sel = PEER_SEL
    n_chunks = d // LANES
    ns = PEER_SLOTS
    step = pl.program_id(0)
    idx_copy = pltpu.make_async_copy(idx_hbm.at[pl.ds(step * (tb * n_sel), tb * n_sel)], idx_smem, isem)
    idx_copy.start()
    idx_copy.wait()

    def row_copy(slot, row, j):
        return pltpu.make_async_copy(uv_hbm.at[row], buf.at[slot, :, j, :], sem.at[slot])

    def issue(t, slot):
        base = t * n_sel
        for j in range(n_sel):
            row_copy(slot, idx_smem[base + j], j).start(priority=j % 2)

    def wait(slot):
        for j in range(n_sel):
            row_copy(slot, 0, j).wait()

    ii = lax.broadcasted_iota(jnp.int32, (n_sel, n_sel), 0)
    jj = lax.broadcasted_iota(jnp.int32, (n_sel, n_sel), 1)
    eye = ii == jj
    high = jnp.uint32(0xFFFF0000)

    def consume(t, slot):
        x_row = x_ref[pl.ds(t, 1), :]
        acc = jnp.zeros((n_sel, LANES), F32)
        for c in range(n_chunks):
            u_c = lax.bitcast_convert_type(buf[slot, c] & high, F32)
            acc += u_c * x_row[:, c * LANES:(c + 1) * LANES]
        hid = jnp.sum(acc, axis=1, keepdims=True)
        g_col = jnp.sum(jnp.where(eye, g_ref[pl.ds(t, 1), :], 0.0), axis=1, keepdims=True)
        act = g_col * (0.5 * hid * (1.0 + _erf(hid * (2.0 ** -0.5))))
        y_chunks = []
        for c in range(n_chunks):
            v_c = lax.bitcast_convert_type(buf[slot, c] << 16, F32)
            y_chunks.append(jnp.sum(v_c * act, axis=0, keepdims=True))
        y_ref[pl.ds(t, 1), :] = jnp.concatenate(y_chunks, axis=1)

    for t0 in range(ns - 1):
        issue(t0, t0)

    def group(k, carry):
        for j in range(ns):
            t = k * ns + j
            issue(t + ns - 1, (j + ns - 1) % ns)
            wait(j)
            consume(t, j)
        return carry

    lax.fori_loop(0, tb // ns - 1, group, 0)
    for j in range(ns):
        t = tb - ns + j
        if j == 0:
            issue(tb - 1, ns - 1)
        wait(j)
        consume(t, j)
    z = alpha * x_ref[...] + y_ref[...]
    mu = jnp.mean(z, axis=-1, keepdims=True)
    zc = z - mu
    var = jnp.mean(zc * zc, axis=-1, keepdims=True)
    y_ref[...] = zc * lax.rsqrt(var + EPS) * lng_ref[...] + lnb_ref[...]


def peer_experts(idx, gates, x, uv, ln_g, ln_b, alpha, *, tb=256):
    t, d = x.shape
    n_sel = PEER_SEL
    return pl.pallas_call(
        functools.partial(_peer_expert_kernel, tb=tb, d=d, alpha=alpha),
        grid=(t // tb,),
        in_specs=[
            pl.BlockSpec(memory_space=pl.ANY),
            pl.BlockSpec((tb, n_sel), lambda i: (i, 0)),
            pl.BlockSpec((tb, d), lambda i: (i, 0)),
            pl.BlockSpec(memory_space=pl.ANY),
            pl.BlockSpec((1, d), lambda i: (0, 0)),
            pl.BlockSpec((1, d), lambda i: (0, 0)),
        ],
        out_specs=pl.BlockSpec((tb, d), lambda i: (i, 0)),
        out_shape=jax.ShapeDtypeStruct((t, d), F32),
        scratch_shapes=[
            pltpu.SMEM((tb * n_sel,), jnp.int32),
            pltpu.VMEM((PEER_SLOTS, d // LANES, n_sel, LANES), jnp.uint32),
            pltpu.SemaphoreType.DMA,
            pltpu.SemaphoreType.DMA((PEER_SLOTS,)),
        ],
        compiler_params=_params("arbitrary"),
        name="peer_experts",
    )(idx, gates, x, uv, ln_g.reshape(1, d), ln_b.reshape(1, d))


def _pack_expert_tables(u, v):
    e, d = u.shape
    hi = lax.bitcast_convert_type(u.astype(BF16), jnp.uint16).astype(jnp.uint32) << 16
    lo = lax.bitcast_convert_type(v.astype(BF16), jnp.uint16).astype(jnp.uint32)
    return (hi | lo).reshape(e, d // LANES, LANES)


def peer_layer(x, w_q, sub_keys, u, v, ln_g, ln_b, alpha):
    t = x.shape[0]
    q_t = matmul_nt(w_q.T.astype(BF16), x)
    experts, gates = peer_select(q_t, sub_keys)
    idx = experts.reshape(PEER_SEL, t).T.reshape(t * PEER_SEL)
    gates = gates.reshape(PEER_SEL, t).T
    return peer_experts(idx, gates, x, _pack_expert_tables(u, v), ln_g, ln_b, alpha)


def _even_mixer(xt, bsz, seq, w_in, conv_w, gate_params, norm_g, q_norm_g, w_uq, kv_norm_g, w_ukv, w_out,
                cos_m, sin_m):
    nh = DN_HEADS
    c_qkvz = 4 * nh * DN_DK
    c_gate = c_qkvz + 4 * nh
    c_cq = c_gate + MLA_Q_RANK
    c_ckv = c_cq + MLA_KV_RANK
    w_gate = w_in[:, c_qkvz:c_gate]
    w_kr = w_in[:, c_ckv:]
    w_all = jnp.concatenate(
        [w_in[:, :c_qkvz], w_in[:, c_gate:c_ckv], w_kr, w_kr, w_gate,
         jnp.zeros((w_in.shape[0], LANES - 4 * nh), w_in.dtype)], axis=1).astype(BF16)
    h_all = matmul(xt, w_all, tn=768)
    cols = w_all.shape[1]
    h3d = h_all.reshape(bsz, seq, cols)
    lat0 = c_qkvz
    kpe_block = (lat0 + MLA_Q_RANK + MLA_KV_RANK) // LANES
    g0 = lat0 + MLA_Q_RANK + MLA_KV_RANK + LANES

    qkv = deltanet_conv(h3d, conv_w, seq)
    n_chunks = seq // DN_CHUNK
    gates = h_all[:, g0:g0 + 4 * nh].reshape(bsz, n_chunks, DN_CHUNK, 4, nh).transpose(0, 4, 3, 1, 2)
    o2 = deltanet_scan(qkv, gates, gate_params, seq)
    o_a = deltanet_gated_norm(o2, h3d, 3 * nh, norm_g, seq)

    hq = MLA_NOPE + MLA_ROPE
    w_uq_r = w_uq.reshape(MLA_Q_RANK, MLA_HEADS, hq)
    w_uq_p = jnp.concatenate([w_uq_r[:, :, :MLA_NOPE].reshape(MLA_Q_RANK, -1),
                              w_uq_r[:, :, MLA_NOPE:].reshape(MLA_Q_RANK, -1)], axis=1).astype(BF16)
    w_ukv_r = w_ukv.reshape(MLA_KV_RANK, MLA_HEADS, MLA_NOPE + MLA_V)
    w_ukv_p = jnp.concatenate([w_ukv_r[:, :, :MLA_NOPE].reshape(MLA_KV_RANK, -1),
                               w_ukv_r[:, :, MLA_NOPE:].reshape(MLA_KV_RANK, -1)], axis=1).astype(BF16)
    q_up = rms_matmul(h_all, lat0 // MLA_Q_RANK, q_norm_g, w_uq_p)
    kv_up = rms_matmul(h_all, (lat0 + MLA_Q_RANK) // MLA_KV_RANK, kv_norm_g, w_ukv_p)
    o_b = mla_attention(q_up.reshape(bsz, seq, -1), kv_up.reshape(bsz, seq, -1), h3d, kpe_block, cos_m, sin_m, seq)

    n_a = nh * DN_DV
    w_out = w_out.astype(BF16)
    return matmul2(o_a.reshape(bsz * seq, n_a), w_out[:n_a], o_b.reshape(bsz * seq, -1), w_out[n_a:])


def _odd_mixer(xt, bsz, seq, w_in, w_out, cos_h, sin_h):
    h3 = matmul(xt, w_in.astype(BF16), tn=768).reshape(bsz, seq, -1)
    pairs = []
    for g, (window, dilation) in enumerate(DIL_GROUPS):
        pairs.append(dilated_group_attention(h3, g, dilation, window // (2 * dilation), cos_h, sin_h, seq))
    o = dilated_combine(pairs)
    return matmul(o, w_out.astype(BF16))


def kernel(x, ab_w_in, a_conv_w, a_log_f, a_dt_bias_f, a_log_b, a_dt_bias_b, a_out_norm_g, b_q_norm_g, b_w_uq,
           b_kv_norm_g, b_w_ukv, ab_w_out, c_w_in, c_w_out, mix_ln_g, mix_ln_b, peer_w_q, peer_sub_keys, peer_u,
           peer_v, ffn_ln_g, ffn_ln_b):
    bsz, seq, d = x.shape
    depth = mix_ln_g.shape[0]
    alpha = (2 * depth) ** 0.25
    cos_h, sin_h = _rope_tables(seq, HEAD_DIM)
    cos_m, sin_m = _rope_tables(seq, MLA_ROPE)
    xt = x.reshape(bsz * seq, d)
    for layer in range(depth):
        i = layer // 2
        if layer % 2 == 0:
            gate_params = jnp.stack([a_log_f[i], a_dt_bias_f[i], a_log_b[i], a_dt_bias_b[i]], axis=1)
            mix = _even_mixer(xt, bsz, seq, ab_w_in[i], a_conv_w[i], gate_params, a_out_norm_g[i], b_q_norm_g[i],
                              b_w_uq[i], b_kv_norm_g[i], b_w_ukv[i], ab_w_out[i], cos_m, sin_m)
        else:
            mix = _odd_mixer(xt, bsz, seq, c_w_in[i], c_w_out[i], cos_h, sin_h)
        xt = add_layer_norm(xt, mix, mix_ln_g[layer], mix_ln_b[layer], alpha)
        xt = peer_layer(xt, peer_w_q[layer], peer_sub_keys[layer], peer_u[layer], peer_v[layer],
                        ffn_ln_g[layer], ffn_ln_b[layer], alpha)
    return xt.reshape(bsz, seq, d)
```

```python
import functools

import jax
import jax.numpy as jnp
from jax import lax
from jax.experimental import pallas as pl
from jax.experimental.pallas import tpu as pltpu

F32 = jnp.float32
BF16 = jnp.bfloat16

HEAD_DIM = 128
ROPE_THETA = 10000.0
EPS = 1e-5
LANES = 128

DN_HEADS = 8
DN_DK = 128
DN_DV = 128
DN_CONV = 5
DN_CHUNK = 64

MLA_HEADS = 8
MLA_Q_RANK = 512
MLA_KV_RANK = 512
MLA_NOPE = 128
MLA_ROPE = 64
MLA_V = 128

DIL_GROUPS = ((128, 1), (512, 4), (2048, 16))
DIL_HEADS = 8

PEER_HEADS = 8
PEER_N_KEYS = 128
PEER_QUERY_DIM = 256
PEER_TOPK = 16
PEER_SEL = PEER_HEADS * PEER_TOPK

VMEM_LIMIT = 48 * 1024 * 1024


def _params(*sem):
    return pltpu.CompilerParams(dimension_semantics=sem, vmem_limit_bytes=VMEM_LIMIT)


def _dot_nt(a, b, **kw):
    return lax.dot_general(a, b, (((1,), (1,)), ((), ())), preferred_element_type=F32, **kw)


def _mm_kernel(a_ref, b_ref, o_ref):
    o_ref[...] = jnp.dot(a_ref[...].astype(BF16), b_ref[...], preferred_element_type=F32).astype(o_ref.dtype)


def matmul(a, b, *, tm=512, tn=512, out_dtype=F32):
    m, k = a.shape
    n = b.shape[1]
    return pl.pallas_call(
        _mm_kernel,
        grid=(n // tn, m // tm),
        in_specs=[pl.BlockSpec((tm, k), lambda j, i: (i, 0)), pl.BlockSpec((k, tn), lambda j, i: (0, j))],
        out_specs=pl.BlockSpec((tm, tn), lambda j, i: (i, j)),
        out_shape=jax.ShapeDtypeStruct((m, n), out_dtype),
        compiler_params=_params("parallel", "parallel"),
        name="matmul",
    )(a, b)


def _mm_ln_kernel(*refs, n_pairs, alpha):
    x_ref, g_ref, b_ref, o_ref = refs[2 * n_pairs:]
    acc = alpha * x_ref[...]
    for p in range(n_pairs):
        acc += jnp.dot(refs[2 * p][...].astype(BF16), refs[2 * p + 1][...], preferred_element_type=F32)
    mu = jnp.mean(acc, axis=-1, keepdims=True)
    zc = acc - mu
    var = jnp.mean(zc * zc, axis=-1, keepdims=True)
    o_ref[...] = zc * lax.rsqrt(var + EPS) * g_ref[...] + b_ref[...]


def matmul_add_layer_norm(pairs, x, g, b, alpha, *, tm=256):
    m, d = x.shape
    flat, specs = [], []
    for a, w in pairs:
        k = a.shape[1]
        flat += [a, w]
        specs += [pl.BlockSpec((tm, k), lambda i: (i, 0)), pl.BlockSpec((k, d), lambda i: (0, 0))]
    row = pl.BlockSpec((tm, d), lambda i: (i, 0))
    vec = pl.BlockSpec((1, d), lambda i: (0, 0))
    return pl.pallas_call(
        functools.partial(_mm_ln_kernel, n_pairs=len(pairs), alpha=alpha),
        grid=(m // tm,),
        in_specs=specs + [row, vec, vec],
        out_specs=row,
        out_shape=jax.ShapeDtypeStruct((m, d), F32),
        compiler_params=_params("parallel"),
        name="matmul_add_layer_norm",
    )(*flat, x, g.reshape(1, d), b.reshape(1, d))


def _mm_nt_kernel(a_ref, b_ref, o_ref):
    o_ref[...] = _dot_nt(a_ref[...], b_ref[...].astype(BF16))


def matmul_nt(a, b, *, tm=512, tn=512):
    m, k = a.shape
    n = b.shape[0]
    return pl.pallas_call(
        _mm_nt_kernel,
        grid=(m // tm, n // tn),
        in_specs=[pl.BlockSpec((tm, k), lambda i, j: (i, 0)), pl.BlockSpec((tn, k), lambda i, j: (j, 0))],
        out_specs=pl.BlockSpec((tm, tn), lambda i, j: (i, j)),
        out_shape=jax.ShapeDtypeStruct((m, n), F32),
        compiler_params=_params("parallel", "parallel"),
        name="matmul_nt",
    )(a, b)


def _rms_mm_kernel(a_ref, g_ref, b_ref, o_ref):
    a = a_ref[...]
    a = a * lax.rsqrt(jnp.mean(a * a, axis=-1, keepdims=True) + EPS) * g_ref[...]
    o_ref[...] = jnp.dot(a.astype(BF16), b_ref[...], preferred_element_type=F32)


def rms_matmul(a, col_block, g, b, *, tm=512, tn=512):
    m = a.shape[0]
    k, n = b.shape
    return pl.pallas_call(
        _rms_mm_kernel,
        grid=(n // tn, m // tm),
        in_specs=[
            pl.BlockSpec((tm, k), lambda j, i: (i, col_block)),
            pl.BlockSpec((1, k), lambda j, i: (0, 0)),
            pl.BlockSpec((k, tn), lambda j, i: (0, j)),
        ],
        out_specs=pl.BlockSpec((tm, tn), lambda j, i: (i, j)),
        out_shape=jax.ShapeDtypeStruct((m, n), F32),
        compiler_params=_params("parallel", "parallel"),
        name="rms_matmul",
    )(a, g.reshape(1, k), b)


def _rope_tables(seq, dim):
    half = dim // 2
    inv_freq = ROPE_THETA ** (-jnp.arange(0, dim, 2, dtype=F32) / dim)
    ang = jnp.arange(seq, dtype=F32)[:, None] * inv_freq[None, :]
    cos, sin = jnp.cos(ang), jnp.sin(ang)
    reps = LANES // dim
    cos_t = jnp.tile(jnp.concatenate([cos, cos], axis=-1), (1, reps))
    sin_t = jnp.tile(jnp.concatenate([-sin, sin], axis=-1), (1, reps))
    return cos_t, sin_t


def _rope(x, cos_t, sin_t, half):
    if 2 * half == LANES:
        partner = pltpu.roll(x, half, 1)
    else:
        lane = lax.broadcasted_iota(jnp.int32, x.shape, 1)
        first = (lane % (2 * half)) < half
        partner = jnp.where(first, pltpu.roll(x, LANES - half, 1), pltpu.roll(x, half, 1))
    return x * cos_t + partner * sin_t


def _conv_kernel(x_ref, w_ref, o_ref, pad_ref, *, seq, width, n_norm, n_scaled, scale):
    c = pl.program_id(1)
    half = (width - 1) // 2
    pad = 8
    pad_ref[0:pad, :] = jnp.zeros((pad, LANES), F32)
    pad_ref[pad + seq:pad + seq + pad, :] = jnp.zeros((pad, LANES), F32)
    pad_ref[pad:pad + seq, :] = x_ref[0]
    acc = jnp.zeros((seq, LANES), F32)
    for j in range(width):
        acc += w_ref[j:j + 1, :] * pad_ref[pl.ds(pad + j - half, seq), :]
    y = acc * jax.nn.sigmoid(acc)
    nrm = y * lax.rsqrt(jnp.sum(y * y, axis=-1, keepdims=True) + 1e-6)
    nrm = nrm * jnp.where(c < n_scaled, scale, 1.0)
    o_ref[0] = jnp.where(c < n_norm, nrm, y)


def deltanet_conv(h_all, conv_w, seq):
    bsz = h_all.shape[0]
    n_ch = conv_w.shape[1]
    width = conv_w.shape[0]
    return pl.pallas_call(
        functools.partial(_conv_kernel, seq=seq, width=width, n_norm=2 * DN_HEADS, n_scaled=DN_HEADS,
                          scale=DN_DK ** -0.5),
        grid=(bsz, n_ch // LANES),
        in_specs=[
            pl.BlockSpec((1, seq, LANES), lambda b, c: (b, 0, c)),
            pl.BlockSpec((width, LANES), lambda b, c: (0, c)),
        ],
        out_specs=pl.BlockSpec((1, seq, LANES), lambda b, c: (b, 0, c)),
        out_shape=jax.ShapeDtypeStruct((bsz, seq, n_ch), F32),
        scratch_shapes=[pltpu.VMEM((seq + 16, LANES), F32)],
        compiler_params=_params("parallel", "parallel"),
        name="deltanet_conv",
    )(h_all, conv_w)


def _delta_chunks(chains):
    def hdot(a, b):
        a_hi = a.astype(BF16)
        b_hi = b.astype(BF16)
        a_lo = (a - a_hi.astype(F32)).astype(BF16)
        b_lo = (b - b_hi.astype(F32)).astype(BF16)
        return (jnp.dot(a_hi, b_hi, preferred_element_type=F32) + jnp.dot(a_hi, b_lo, preferred_element_type=F32)
                + jnp.dot(a_lo, b_hi, preferred_element_type=F32))

    dot = functools.partial(jnp.dot, preferred_element_type=F32)
    chunk = chains[0]["qc"].shape[0]
    for c in chains:
        g_row = c["neg_a"] * jax.nn.softplus(c["a_row"] + c["dt_bias"])
        beta_row = jax.nn.sigmoid(c["b_row"])
        gc_col = jnp.sum(jnp.where(c["lower"], g_row, 0.0), axis=1, keepdims=True)
        gc_row = jnp.sum(jnp.where(c["eye"], gc_col, 0.0), axis=0, keepdims=True)
        beta_col = jnp.sum(jnp.where(c["eye"], beta_row, 0.0), axis=1, keepdims=True)
        c["decay"] = jnp.where(c["lower"], jnp.exp(jnp.minimum(gc_col - gc_row, 0.0)), 0.0)
        c["kb"] = c["kc"] * beta_col
        c["vb"] = c["vc"] * beta_col
        c["egc"] = jnp.exp(gc_col)
        c["g_last"] = jnp.sum(g_row, axis=1, keepdims=True)
        c["k_dec"] = c["kc"] * jnp.exp(c["g_last"] - gc_col)
    for c in chains:
        c["kk"] = _dot_nt(c["kb"], c["kc"])
    for c in chains:
        c["qk"] = _dot_nt(c["qc"], c["kc"])
    for c in chains:
        c["xp"] = -jnp.where(c["strict"], c["kk"] * c["decay"], 0.0)
        c["tinv"] = c["eye"].astype(F32) + c["xp"]
    span = 2
    while span < chunk:
        for c in chains:
            c["xp"] = hdot(c["xp"], c["xp"])
        for c in chains:
            c["tinv"] = c["tinv"] + hdot(c["tinv"], c["xp"])
        span *= 2
    for c in chains:
        c["uw"] = hdot(c["tinv"], jnp.concatenate([c["vb"], c["kb"] * c["egc"]], axis=1))
    for c in chains:
        dv = c["vb"].shape[1]
        c["v_new"] = c["uw"][:, :dv] - dot(c["uw"][:, dv:], c["state"])
    for c in chains:
        c["out"] = dot(c["qc"] * c["egc"], c["state"])
    for c in chains:
        c["out"] = c["out"] + dot(c["qk"] * c["decay"], c["v_new"])
    for c in chains:
        c["new_state"] = c["state"] * jnp.exp(c["g_last"]) + dot(c["k_dec"].T, c["v_new"])
    return [(c["out"], c["new_state"]) for c in chains]


def _delta_kernel(p_ref, q_ref, k_ref, v_ref, gt_ref, o_ref, s_ref, *, seq, chunk, heads_per_step):
    hg = pl.program_id(1)
    n_chunks = seq // chunk
    ii = lax.broadcasted_iota(jnp.int32, (chunk, chunk), 0)
    jj = lax.broadcasted_iota(jnp.int32, (chunk, chunk), 1)
    eye = ii == jj
    masks = ((ii >= jj, ii > jj), (ii <= jj, ii < jj))
    neg_a, dt_bias = [], []
    for hh in range(heads_per_step):
        for d in range(2):
            h = hg * heads_per_step + hh
            neg_a.append(-jnp.exp(jnp.full((1, chunk), p_ref[h, 2 * d], F32)))
            dt_bias.append(p_ref[h, 2 * d + 1])
    s_ref[...] = jnp.zeros_like(s_ref)

    def body(n, carry):
        chains, where = [], []
        for hh in range(heads_per_step):
            cols = slice(hh * LANES, (hh + 1) * LANES)
            for d in range(2):
                ch = 2 * hh + d
                c = n if d == 0 else n_chunks - 1 - n
                r0 = pl.multiple_of(c * chunk, chunk)
                rows = pl.ds(r0, chunk)
                chains.append(dict(
                    qc=q_ref[0, rows, cols], kc=k_ref[0, rows, cols], vc=v_ref[0, rows, cols],
                    a_row=gt_ref[0, hh, 2 * d, pl.ds(c, 1), :], b_row=gt_ref[0, hh, 2 * d + 1, pl.ds(c, 1), :],
                    neg_a=neg_a[ch], dt_bias=dt_bias[ch], lower=masks[d][0], strict=masks[d][1], eye=eye,
                    state=s_ref[ch]))
                where.append((ch, d, rows, cols))
        for (ch, d, rows, cols), (out, new_state) in zip(where, _delta_chunks(chains)):
            s_ref[ch] = new_state
            o_ref[d, 0, rows, cols] = out
        return carry

    lax.fori_loop(0, n_chunks, body, 0)


def deltanet_scan(qkv, gates, gate_params, seq, *, heads_per_step=4):
    bsz = qkv.shape[0]
    nh = DN_HEADS
    chunk = DN_CHUNK
    n_chunks = seq // chunk
    hps = heads_per_step
    groups = nh // hps
    width = hps * LANES
    return pl.pallas_call(
        functools.partial(_delta_kernel, seq=seq, chunk=chunk, heads_per_step=hps),
        grid=(bsz, groups),
        in_specs=[
            pl.BlockSpec(memory_space=pltpu.SMEM),
            pl.BlockSpec((1, seq, width), lambda b, g: (b, 0, g)),
            pl.BlockSpec((1, seq, width), lambda b, g: (b, 0, groups + g)),
            pl.BlockSpec((1, seq, width), lambda b, g: (b, 0, 2 * groups + g)),
            pl.BlockSpec((1, hps, 4, n_chunks, chunk), lambda b, g: (b, g, 0, 0, 0)),
        ],
        out_specs=pl.BlockSpec((2, 1, seq, width), lambda b, g: (0, b, 0, g)),
        out_shape=jax.ShapeDtypeStruct((2, bsz, seq, nh * DN_DV), F32),
        scratch_shapes=[pltpu.VMEM((2 * hps, DN_DK, DN_DV), F32)],
        compiler_params=_params("parallel", "parallel"),
        name="deltanet_scan",
    )(gate_params, qkv, qkv, qkv, gates)


def _gated_norm_kernel(of_ref, ob_ref, z_ref, g_ref, o_ref):
    o = of_ref[0, 0] + ob_ref[0, 0]
    o = o * lax.rsqrt(jnp.mean(o * o, axis=-1, keepdims=True) + EPS) * g_ref[...]
    z = z_ref[0]
    o_ref[0] = o * (z * jax.nn.sigmoid(z))


def deltanet_gated_norm(o2, h_all, z_block0, norm_g, seq, *, ts=512):
    bsz = o2.shape[1]
    nh = DN_HEADS
    return pl.pallas_call(
        _gated_norm_kernel,
        grid=(bsz, seq // ts, nh),
        in_specs=[
            pl.BlockSpec((1, 1, ts, LANES), lambda b, i, h: (0, b, i, h)),
            pl.BlockSpec((1, 1, ts, LANES), lambda b, i, h: (1, b, i, h)),
            pl.BlockSpec((1, ts, LANES), lambda b, i, h: (b, i, z_block0 + h)),
            pl.BlockSpec((1, LANES), lambda b, i, h: (0, 0)),
        ],
        out_specs=pl.BlockSpec((1, ts, LANES), lambda b, i, h: (b, i, h)),
        out_shape=jax.ShapeDtypeStruct((bsz, seq, nh * DN_DV), F32),
        compiler_params=_params("parallel", "parallel", "parallel"),
        name="deltanet_gated_norm",
    )(o2, o2, h_all, norm_g.reshape(1, DN_DV))


def _mla_attn_kernel(qn_ref, qpe_ref, kn_ref, kpe_ref, v_ref, cq_ref, sq_ref, ck_ref, sk_ref, o_ref, *, scale):
    h = pl.program_id(1)
    half = MLA_ROPE // 2
    qpe = _rope(qpe_ref[0], cq_ref[...], sq_ref[...], half)
    lane = lax.broadcasted_iota(jnp.int32, qpe.shape, 1)
    qpe = jnp.where((lane // MLA_ROPE) == (h % 2), qpe, 0.0)
    kpe = _rope(kpe_ref[0], ck_ref[...], sk_ref[...], half)
    s = _dot_nt(qn_ref[0].astype(BF16), kn_ref[0].astype(BF16))
    s += _dot_nt(qpe.astype(BF16), kpe.astype(BF16))
    s = s * scale
    m = jnp.max(s, axis=-1, keepdims=True)
    p = jnp.exp(s - m)
    den = jnp.sum(p, axis=-1, keepdims=True)
    o = jnp.dot(p.astype(BF16), v_ref[0].astype(BF16), preferred_element_type=F32)
    o_ref[0] = o / den


def mla_attention(q_up, kv_up, h_all, kpe_block, cos_t, sin_t, seq, *, tq=512):
    bsz = q_up.shape[0]
    nh = MLA_HEADS
    scale = (MLA_NOPE + MLA_ROPE) ** -0.5
    return pl.pallas_call(
        functools.partial(_mla_attn_kernel, scale=scale),
        grid=(bsz, nh, seq // tq),
        in_specs=[
            pl.BlockSpec((1, tq, LANES), lambda b, h, i: (b, i, h)),
            pl.BlockSpec((1, tq, LANES), lambda b, h, i: (b, i, nh + h // 2)),
            pl.BlockSpec((1, seq, LANES), lambda b, h, i: (b, 0, h)),
            pl.BlockSpec((1, seq, LANES), lambda b, h, i: (b, 0, kpe_block)),
            pl.BlockSpec((1, seq, LANES), lambda b, h, i: (b, 0, nh + h)),
            pl.BlockSpec((tq, LANES), lambda b, h, i: (i, 0)),
            pl.BlockSpec((tq, LANES), lambda b, h, i: (i, 0)),
            pl.BlockSpec((seq, LANES), lambda b, h, i: (0, 0)),
            pl.BlockSpec((seq, LANES), lambda b, h, i: (0, 0)),
        ],
        out_specs=pl.BlockSpec((1, tq, LANES), lambda b, h, i: (b, i, h)),
        out_shape=jax.ShapeDtypeStruct((bsz, seq, nh * MLA_V), F32),
        compiler_params=_params("parallel", "parallel", "parallel"),
        name="mla_attention",
    )(q_up, q_up, kv_up, h_all, kv_up, cos_t, sin_t, cos_t, sin_t)


def _band_attention_multi(qs, ks, vs, q0, k0, radius, scale):
    ss = [_dot_nt(q.astype(BF16), k.astype(BF16)) * scale for q, k in zip(qs, ks)]
    qi = q0 + lax.broadcasted_iota(jnp.int32, ss[0].shape, 0)
    kj = k0 + lax.broadcasted_iota(jnp.int32, ss[0].shape, 1)
    band = jnp.abs(qi - kj) <= radius
    ss = [jnp.where(band, s, -jnp.inf) for s in ss]
    ms = [jnp.max(s, axis=-1, keepdims=True) for s in ss]
    ps = [jnp.exp(s - m) for s, m in zip(ss, ms)]
    dens = [jnp.sum(p, axis=-1, keepdims=True) for p in ps]
    os_ = [jnp.dot(p.astype(BF16), v.astype(BF16), preferred_element_type=F32) for p, v in zip(ps, vs)]
    return [(o / den, m + jnp.log(den)) for o, den, m in zip(os_, dens, ms)]


def _dil_attn_kernel(q_ref, k_ref, v_ref, cos_ref, sin_ref, o_ref, lse_ref, kr_ref, *, seq, dilation, radius, scale,
                     par):
    half = HEAD_DIM // 2
    length = seq // dilation
    tq = min(length, 256)
    tk = min(length, tq + 4 * radius)

    def load(r):
        rows = pl.ds(r, length, stride=dilation) if dilation > 1 else pl.ds(0, length)
        cos, sin = cos_ref[rows, :], sin_ref[rows, :]
        return rows, _rope(q_ref[0, rows, :], cos, sin, half), _rope(k_ref[0, rows, :], cos, sin, half)

    if tk == length:
        def body(g, carry):
            loaded = [load(g * par + i) for i in range(par)]
            vs = [v_ref[0, rows, :] for rows, _, _ in loaded]
            res = _band_attention_multi([q for _, q, _ in loaded], [k for _, _, k in loaded], vs, 0, 0, radius, scale)
            for (rows, _, _), (o, lse) in zip(loaded, res):
                o_ref[0, rows, :] = o
                lse_ref[0, rows, :] = jnp.broadcast_to(lse, o.shape)
            return carry
        lax.fori_loop(0, dilation // par, body, 0)
    else:
        _, q, k = load(0)
        kr_ref[...] = k
        for i in range(length // tq):
            k0 = min(max(i * tq - 2 * radius, 0), length - tk)
            (o, lse), = _band_attention_multi([q[i * tq:(i + 1) * tq]], [kr_ref[k0:k0 + tk, :]],
                                              [v_ref[0, k0:k0 + tk, :]], i * tq, k0, radius, scale)
            o_ref[0, i * tq:(i + 1) * tq, :] = o
            lse_ref[0, i * tq:(i + 1) * tq, :] = jnp.broadcast_to(lse, o.shape)


def dilated_group_attention(h3, group, dilation, radius, cos_t, sin_t, seq):
    bsz = h3.shape[0]
    nh = DIL_HEADS
    base = group * 3 * nh
    out_sds = jax.ShapeDtypeStruct((bsz, seq, nh * HEAD_DIM), F32)
    blk = lambda off: pl.BlockSpec((1, seq, LANES), lambda b, h: (b, 0, base + off + h))
    tbl = pl.BlockSpec((seq, LANES), lambda b, h: (0, 0))
    out = pl.BlockSpec((1, seq, LANES), lambda b, h: (b, 0, h))
    o, lse = pl.pallas_call(
        functools.partial(_dil_attn_kernel, seq=seq, dilation=dilation, radius=radius, scale=HEAD_DIM ** -0.5,
                          par=min(dilation, 4)),
        grid=(bsz, nh),
        in_specs=[blk(0), blk(nh), blk(2 * nh), tbl, tbl],
        out_specs=[out, out],
        out_shape=[out_sds, out_sds],
        scratch_shapes=[pltpu.VMEM((seq // dilation, LANES), F32)],
        compiler_params=_params("parallel", "parallel"),
        name=f"dilated_attention_g{group}",
    )(h3, h3, h3, cos_t, sin_t)
    return o.reshape(bsz * seq, nh * HEAD_DIM), lse.reshape(bsz * seq, nh * HEAD_DIM)


def _dil_combine_kernel(*refs):
    o_ref = refs[-1]
    n = (len(refs) - 1) // 2
    outs = [refs[2 * g][...] for g in range(n)]
    lses = [refs[2 * g + 1][...] for g in range(n)]
    m = functools.reduce(jnp.maximum, lses)
    es = [jnp.exp(l - m) for l in lses]
    num = functools.reduce(jnp.add, [e * o for e, o in zip(es, outs)])
    o_ref[...] = num / functools.reduce(jnp.add, es)


def dilated_combine(pairs, *, tm=256):
    m, d = pairs[0][0].shape
    flat = [a for pair in pairs for a in pair]
    spec = pl.BlockSpec((tm, d), lambda i: (i, 0))
    return pl.pallas_call(
        _dil_combine_kernel,
        grid=(m // tm,),
        in_specs=[spec] * len(flat),
        out_specs=spec,
        out_shape=jax.ShapeDtypeStruct((m, d), F32),
        compiler_params=_params("parallel"),
        name="dilated_combine",
    )(*flat)


def _top_rows(s, k, payload=None):
    rows = s.shape[0]
    idx = lax.broadcasted_iota(jnp.int32, s.shape, 0).astype(F32)
    vals, inds, pays = [], [], []
    for _ in range(k):
        m = jnp.max(s, axis=0, keepdims=True)
        sel = jnp.min(jnp.where(s == m, idx, float(rows)), axis=0, keepdims=True)
        hit = idx == sel
        vals.append(m)
        inds.append(sel)
        if payload is not None:
            pays.append(jnp.sum(jnp.where(hit, payload, 0.0), axis=0, keepdims=True))
        s = jnp.where(hit, -jnp.inf, s)
    cat = lambda xs: jnp.concatenate(xs, axis=0)
    return cat(vals), cat(inds), (cat(pays) if payload is not None else None)


def _peer_select_kernel(qt_ref, keys_ref, e_ref, g_ref):
    kk = PEER_TOPK
    half = PEER_QUERY_DIM // 2
    for lb in range(qt_ref.shape[1] // LANES):
        lanes = slice(lb * LANES, (lb + 1) * LANES)
        tops = []
        for p in range(2):
            s = jnp.dot(keys_ref[p], qt_ref[p * half:(p + 1) * half, lanes], preferred_element_type=F32)
            tops.append(_top_rows(s, kk)[:2])
        (s0, i0), (s1, i1) = tops
        sub = lax.broadcasted_iota(jnp.int32, (8, LANES), 0)
        pieces_s, pieces_e = [], []
        for a in range(kk // 2):
            nb = kk // (a + 1)
            rows = kk if nb > 8 else 8
            ps = s0[a:a + 1] + s1[0:rows]
            if nb < rows:
                ps = jnp.where(sub < nb, ps, -jnp.inf)
            pieces_s.append(ps)
            pieces_e.append(i0[a:a + 1] * float(PEER_N_KEYS) + i1[0:rows])
        pieces_s.append(s0[kk // 2:kk] + s1[0:1])
        pieces_e.append(i0[kk // 2:kk] * float(PEER_N_KEYS) + i1[0:1])
        cand_s = jnp.concatenate(pieces_s, axis=0)
        cand_e = jnp.concatenate(pieces_e, axis=0)
        best_s, _, experts = _top_rows(cand_s, kk, payload=cand_e)
        ex = jnp.exp(best_s - best_s[0:1])
        e_ref[0, :, lanes] = experts.astype(jnp.int32)
        g_ref[0, :, lanes] = ex / jnp.sum(ex, axis=0, keepdims=True)


def peer_select(q_t, sub_keys, *, tt=512):
    t = q_t.shape[1]
    nh = PEER_HEADS
    kk = PEER_TOPK
    return pl.pallas_call(
        _peer_select_kernel,
        grid=(t // tt, nh),
        in_specs=[
            pl.BlockSpec((PEER_QUERY_DIM, tt), lambda i, h: (h, i)),
            pl.BlockSpec((2, PEER_N_KEYS, PEER_QUERY_DIM // 2), lambda i, h: (0, 0, 0)),
        ],
        out_specs=[
            pl.BlockSpec((1, kk, tt), lambda i, h: (h, 0, i)),
            pl.BlockSpec((1, kk, tt), lambda i, h: (h, 0, i)),
        ],
        out_shape=[jax.ShapeDtypeStruct((nh, kk, t), jnp.int32), jax.ShapeDtypeStruct((nh, kk, t), F32)],
        compiler_params=_params("parallel", "parallel"),
        name="peer_select",
    )(q_t, sub_keys)


def _erf(x):
    x = jnp.clip(x, -4.0, 4.0)
    x2 = x * x
    alpha = (-2.72614225801306e-10, 2.77068142495902e-08, -2.10102402082508e-06, -5.69250639462346e-05,
             -7.34990630326855e-04, -2.95459980854025e-03, -1.60960333262415e-02)
    beta = (-1.45660718464996e-05, -2.13374055278905e-04, -1.68282697438203e-03, -7.37332916720468e-03,
            -1.42647390514189e-02)
    p = jnp.full_like(x, alpha[0])
    for c in alpha[1:]:
        p = p * x2 + c
    q = jnp.full_like(x, beta[0])
    for c in beta[1:]:
        q = q * x2 + c
    return x * p / q


PEER_SLOTS = 4


def _peer_expert_kernel(idx_hbm, g_ref, x_ref, uv_hbm, lng_ref, lnb_ref, y_ref, idx_smem, buf, isem, sem, *, tb, d, alpha):
    n_sel = PEER_SEL
    n_chunks = d // LANES
    ns = PEER_SLOTS
    step = pl.program_id(0)
    idx_copy = pltpu.make_async_copy(idx_hbm.at[pl.ds(step * (tb * n_sel), tb * n_sel)], idx_smem, isem)
    idx_copy.start()
    idx_copy.wait()

    def row_copy(slot, row, j):
        return pltpu.make_async_copy(uv_hbm.at[row], buf.at[slot, :, j, :], sem.at[slot])

    def issue(t, slot):
        base = t * n_sel
        for j in range(n_sel):
            row_copy(slot, idx_smem[base + j], j).start(priority=j % 2)

    def wait(slot):
        for j in range(n_sel):
            row_copy(slot, 0, j).wait()

    ii = lax.broadcasted_iota(jnp.int32, (n_sel, n_sel), 0)
    jj = lax.broadcasted_iota(jnp.int32, (n_sel, n_sel), 1)
    eye = ii == jj
    high = jnp.uint32(0xFFFF0000)

    def consume(t, slot):
        x_row = x_ref[pl.ds(t, 1), :]
        acc = jnp.zeros((n_sel, LANES), F32)
        for c in range(n_chunks):
            u_c = lax.bitcast_convert_type(buf[slot, c] & high, F32)
            acc += u_c * x_row[:, c * LANES:(c + 1) * LANES]
        hid = jnp.sum(acc, axis=1, keepdims=True)
        g_col = jnp.sum(jnp.where(eye, g_ref[pl.ds(t, 1), :], 0.0), axis=1, keepdims=True)
        act = g_col * (0.5 * hid * (1.0 + _erf(hid * (2.0 ** -0.5))))
        y_chunks = []
        for c in range(n_chunks):
            v_c = lax.bitcast_convert_type(buf[slot, c] << 16, F32)
            y_chunks.append(jnp.sum(v_c * act, axis=0, keepdims=True))
        y_ref[pl.ds(t, 1), :] = jnp.concatenate(y_chunks, axis=1)

    for t0 in range(ns - 1):
        issue(t0, t0)

    def group(k, carry):
        for j in range(ns):
            t = k * ns + j
            issue(t + ns - 1, (j + ns - 1) % ns)
            wait(j)
            consume(t, j)
        return carry

    lax.fori_loop(0, tb // ns - 1, group, 0)
    for j in range(ns):
        t = tb - ns + j
        if j == 0:
            issue(tb - 1, ns - 1)
        wait(j)
        consume(t, j)
    z = alpha * x_ref[...] + y_ref[...]
    mu = jnp.mean(z, axis=-1, keepdims=True)
    zc = z - mu
    var = jnp.mean(zc * zc, axis=-1, keepdims=True)
    y_ref[...] = zc * lax.rsqrt(var + EPS) * lng_ref[...] + lnb_ref[...]


def peer_experts(idx, gates, x, uv, ln_g, ln_b, alpha, *, tb=256):
    t, d = x.shape
    n_sel = PEER_SEL
    return pl.pallas_call(
        functools.partial(_peer_expert_kernel, tb=tb, d=d, alpha=alpha),
        grid=(t // tb,),
        in_specs=[
            pl.BlockSpec(memory_space=pl.ANY),
            pl.BlockSpec((tb, n_sel), lambda i: (i, 0)),
            pl.BlockSpec((tb, d), lambda i: (i, 0)),
            pl.BlockSpec(memory_space=pl.ANY),
            pl.BlockSpec((1, d), lambda i: (0, 0)),
            pl.BlockSpec((1, d), lambda i: (0, 0)),
        ],
        out_specs=pl.BlockSpec((tb, d), lambda i: (i, 0)),
        out_shape=jax.ShapeDtypeStruct((t, d), F32),
        scratch_shapes=[
            pltpu.SMEM((tb * n_sel,), jnp.int32),
            pltpu.VMEM((PEER_SLOTS, d // LANES, n_sel, LANES), jnp.uint32),
            pltpu.SemaphoreType.DMA,
            pltpu.SemaphoreType.DMA((PEER_SLOTS,)),
        ],
        compiler_params=_params("arbitrary"),
        name="peer_experts",
    )(idx, gates, x, uv, ln_g.reshape(1, d), ln_b.reshape(1, d))


def _pack_expert_tables(u, v):
    e, d = u.shape
    hi = lax.bitcast_convert_type(u.astype(BF16), jnp.uint16).astype(jnp.uint32) << 16
    lo = lax.bitcast_convert_type(v.astype(BF16), jnp.uint16).astype(jnp.uint32)
    return (hi | lo).reshape(e, d // LANES, LANES)


def peer_layer(x, w_q, sub_keys, u, v, ln_g, ln_b, alpha):
    t = x.shape[0]
    q_t = matmul_nt(w_q.T.astype(BF16), x)
    experts, gates = peer_select(q_t, sub_keys)
    idx = experts.reshape(PEER_SEL, t).T.reshape(t * PEER_SEL)
    gates = gates.reshape(PEER_SEL, t).T
    return peer_experts(idx, gates, x, _pack_expert_tables(u, v), ln_g, ln_b, alpha)


def _even_mixer(xt, bsz, seq, w_in, conv_w, gate_params, norm_g, q_norm_g, w_uq, kv_norm_g, w_ukv, w_out,
                cos_m, sin_m):
    nh = DN_HEADS
    c_qkvz = 4 * nh * DN_DK
    c_gate = c_qkvz + 4 * nh
    c_cq = c_gate + MLA_Q_RANK
    c_ckv = c_cq + MLA_KV_RANK
    w_gate = w_in[:, c_qkvz:c_gate]
    w_kr = w_in[:, c_ckv:]
    w_all = jnp.concatenate(
        [w_in[:, :c_qkvz], w_in[:, c_gate:c_ckv], w_kr, w_kr, w_gate,
         jnp.zeros((w_in.shape[0], LANES - 4 * nh), w_in.dtype)], axis=1).astype(BF16)
    h_all = matmul(xt, w_all, tn=768)
    cols = w_all.shape[1]
    h3d = h_all.reshape(bsz, seq, cols)
    lat0 = c_qkvz
    kpe_block = (lat0 + MLA_Q_RANK + MLA_KV_RANK) // LANES
    g0 = lat0 + MLA_Q_RANK + MLA_KV_RANK + LANES

    qkv = deltanet_conv(h3d, conv_w, seq)
    n_chunks = seq // DN_CHUNK
    gates = h_all[:, g0:g0 + 4 * nh].reshape(bsz, n_chunks, DN_CHUNK, 4, nh).transpose(0, 4, 3, 1, 2)
    o2 = deltanet_scan(qkv, gates, gate_params, seq)
    o_a = deltanet_gated_norm(o2, h3d, 3 * nh, norm_g, seq)

    hq = MLA_NOPE + MLA_ROPE
    w_uq_r = w_uq.reshape(MLA_Q_RANK, MLA_HEADS, hq)
    w_uq_p = jnp.concatenate([w_uq_r[:, :, :MLA_NOPE].reshape(MLA_Q_RANK, -1),
                              w_uq_r[:, :, MLA_NOPE:].reshape(MLA_Q_RANK, -1)], axis=1).astype(BF16)
    w_ukv_r = w_ukv.reshape(MLA_KV_RANK, MLA_HEADS, MLA_NOPE + MLA_V)
    w_ukv_p = jnp.concatenate([w_ukv_r[:, :, :MLA_NOPE].reshape(MLA_KV_RANK, -1),
                               w_ukv_r[:, :, MLA_NOPE:].reshape(MLA_KV_RANK, -1)], axis=1).astype(BF16)
    q_up = rms_matmul(h_all, lat0 // MLA_Q_RANK, q_norm_g, w_uq_p)
    kv_up = rms_matmul(h_all, (lat0 + MLA_Q_RANK) // MLA_KV_RANK, kv_norm_g, w_ukv_p)
    o_b = mla_attention(q_up.reshape(bsz, seq, -1), kv_up.reshape(bsz, seq, -1), h3d, kpe_block, cos_m, sin_m, seq)

    n_a = nh * DN_DV
    w_out = w_out.astype(BF16)
    return [(o_a.reshape(bsz * seq, n_a), w_out[:n_a]), (o_b.reshape(bsz * seq, -1), w_out[n_a:])]


def _odd_mixer(xt, bsz, seq, w_in, w_out, cos_h, sin_h):
    h3 = matmul(xt, w_in.astype(BF16), tn=768).reshape(bsz, seq, -1)
    pairs = []
    for g, (window, dilation) in enumerate(DIL_GROUPS):
        pairs.append(dilated_group_attention(h3, g, dilation, window // (2 * dilation), cos_h, sin_h, seq))
    return [(dilated_combine(pairs), w_out.astype(BF16))]


def kernel(x, ab_w_in, a_conv_w, a_log_f, a_dt_bias_f, a_log_b, a_dt_bias_b, a_out_norm_g, b_q_norm_g, b_w_uq,
           b_kv_norm_g, b_w_ukv, ab_w_out, c_w_in, c_w_out, mix_ln_g, mix_ln_b, peer_w_q, peer_sub_keys, peer_u,
           peer_v, ffn_ln_g, ffn_ln_b):
    bsz, seq, d = x.shape
    depth = mix_ln_g.shape[0]
    alpha = (2 * depth) ** 0.25
    cos_h, sin_h = _rope_tables(seq, HEAD_DIM)
    cos_m, sin_m = _rope_tables(seq, MLA_ROPE)
    xt = x.reshape(bsz * seq, d)
    for layer in range(depth):
        i = layer // 2
        if layer % 2 == 0:
            gate_params = jnp.stack([a_log_f[i], a_dt_bias_f[i], a_log_b[i], a_dt_bias_b[i]], axis=1)
            mix = _even_mixer(xt, bsz, seq, ab_w_in[i], a_conv_w[i], gate_params, a_out_norm_g[i], b_q_norm_g[i],
                              b_w_uq[i], b_kv_norm_g[i], b_w_ukv[i], ab_w_out[i], cos_m, sin_m)
        else:
            mix = _odd_mixer(xt, bsz, seq, c_w_in[i], c_w_out[i], cos_h, sin_h)
        xt = matmul_add_layer_norm(mix, xt, mix_ln_g[layer], mix_ln_b[layer], alpha)
        xt = peer_layer(xt, peer_w_q[layer], peer_sub_keys[layer], peer_u[layer], peer_v[layer],
                        ffn_ln_g[layer], ffn_ln_b[layer], alpha)
    return xt.reshape(bsz, seq, d)
```

```python
import functools

import jax
import jax.numpy as jnp
from jax import lax
from jax.experimental import pallas as pl
from jax.experimental.pallas import tpu as pltpu

F32 = jnp.float32
BF16 = jnp.bfloat16

HEAD_DIM = 128
ROPE_THETA = 10000.0
EPS = 1e-5
LANES = 128

DN_HEADS = 8
DN_DK = 128
DN_DV = 128
DN_CONV = 5
DN_CHUNK = 64

MLA_HEADS = 8
MLA_Q_RANK = 512
MLA_KV_RANK = 512
MLA_NOPE = 128
MLA_ROPE = 64
MLA_V = 128

DIL_GROUPS = ((128, 1), (512, 4), (2048, 16))
DIL_HEADS = 8

PEER_HEADS = 8
PEER_N_KEYS = 128
PEER_QUERY_DIM = 256
PEER_TOPK = 16
PEER_SEL = PEER_HEADS * PEER_TOPK

VMEM_LIMIT = 48 * 1024 * 1024


def _params(*sem):
    return pltpu.CompilerParams(dimension_semantics=sem, vmem_limit_bytes=VMEM_LIMIT)


def _dot_nt(a, b, **kw):
    return lax.dot_general(a, b, (((1,), (1,)), ((), ())), preferred_element_type=F32, **kw)


def _mm_kernel(a_ref, b_ref, o_ref):
    o_ref[...] = jnp.dot(a_ref[...].astype(BF16), b_ref[...], preferred_element_type=F32).astype(o_ref.dtype)


def matmul(a, b, *, tm=1024, tn=512, out_dtype=F32):
    m, k = a.shape
    n = b.shape[1]
    return pl.pallas_call(
        _mm_kernel,
        grid=(n // tn, m // tm),
        in_specs=[pl.BlockSpec((tm, k), lambda j, i: (i, 0)), pl.BlockSpec((k, tn), lambda j, i: (0, j))],
        out_specs=pl.BlockSpec((tm, tn), lambda j, i: (i, j)),
        out_shape=jax.ShapeDtypeStruct((m, n), out_dtype),
        compiler_params=_params("parallel", "parallel"),
        name="matmul",
    )(a, b)


def _mm_ln_kernel(*refs, n_pairs, alpha):
    x_ref, g_ref, b_ref, o_ref = refs[2 * n_pairs:]
    acc = alpha * x_ref[...]
    for p in range(n_pairs):
        acc += jnp.dot(refs[2 * p][...].astype(BF16), refs[2 * p + 1][...], preferred_element_type=F32)
    mu = jnp.mean(acc, axis=-1, keepdims=True)
    zc = acc - mu
    var = jnp.mean(zc * zc, axis=-1, keepdims=True)
    o_ref[...] = zc * lax.rsqrt(var + EPS) * g_ref[...] + b_ref[...]


def matmul_add_layer_norm(pairs, x, g, b, alpha, *, tm=512):
    m, d = x.shape
    flat, specs = [], []
    for a, w in pairs:
        k = a.shape[1]
        flat += [a, w]
        specs += [pl.BlockSpec((tm, k), lambda i: (i, 0)), pl.BlockSpec((k, d), lambda i: (0, 0))]
    row = pl.BlockSpec((tm, d), lambda i: (i, 0))
    vec = pl.BlockSpec((1, d), lambda i: (0, 0))
    return pl.pallas_call(
        functools.partial(_mm_ln_kernel, n_pairs=len(pairs), alpha=alpha),
        grid=(m // tm,),
        in_specs=specs + [row, vec, vec],
        out_specs=row,
        out_shape=jax.ShapeDtypeStruct((m, d), F32),
        compiler_params=_params("parallel"),
        name="matmul_add_layer_norm",
    )(*flat, x, g.reshape(1, d), b.reshape(1, d))


def _mm_nt_kernel(a_ref, b_ref, o_ref):
    o_ref[...] = _dot_nt(a_ref[...], b_ref[...].astype(BF16))


def matmul_nt(a, b, *, tm=2048, tn=512):
    m, k = a.shape
    n = b.shape[0]
    return pl.pallas_call(
        _mm_nt_kernel,
        grid=(m // tm, n // tn),
        in_specs=[pl.BlockSpec((tm, k), lambda i, j: (i, 0)), pl.BlockSpec((tn, k), lambda i, j: (j, 0))],
        out_specs=pl.BlockSpec((tm, tn), lambda i, j: (i, j)),
        out_shape=jax.ShapeDtypeStruct((m, n), F32),
        compiler_params=_params("parallel", "parallel"),
        name="matmul_nt",
    )(a, b)


def _rms_mm_kernel(a_ref, g_ref, b_ref, o_ref):
    a = a_ref[...]
    a = a * lax.rsqrt(jnp.mean(a * a, axis=-1, keepdims=True) + EPS) * g_ref[...]
    o_ref[...] = jnp.dot(a.astype(BF16), b_ref[...], preferred_element_type=F32)


def rms_matmul(a, col_block, g, b, *, tm=1024, tn=512):
    m = a.shape[0]
    k, n = b.shape
    return pl.pallas_call(
        _rms_mm_kernel,
        grid=(n // tn, m // tm),
        in_specs=[
            pl.BlockSpec((tm, k), lambda j, i: (i, col_block)),
            pl.BlockSpec((1, k), lambda j, i: (0, 0)),
            pl.BlockSpec((k, tn), lambda j, i: (0, j)),
        ],
        out_specs=pl.BlockSpec((tm, tn), lambda j, i: (i, j)),
        out_shape=jax.ShapeDtypeStruct((m, n), F32),
        compiler_params=_params("parallel", "parallel"),
        name="rms_matmul",
    )(a, g.reshape(1, k), b)


def _rope_tables(seq, dim):
    half = dim // 2
    inv_freq = ROPE_THETA ** (-jnp.arange(0, dim, 2, dtype=F32) / dim)
    ang = jnp.arange(seq, dtype=F32)[:, None] * inv_freq[None, :]
    cos, sin = jnp.cos(ang), jnp.sin(ang)
    reps = LANES // dim
    cos_t = jnp.tile(jnp.concatenate([cos, cos], axis=-1), (1, reps))
    sin_t = jnp.tile(jnp.concatenate([-sin, sin], axis=-1), (1, reps))
    return cos_t, sin_t


def _rope(x, cos_t, sin_t, half):
    if 2 * half == LANES:
        partner = pltpu.roll(x, half, 1)
    else:
        lane = lax.broadcasted_iota(jnp.int32, x.shape, 1)
        first = (lane % (2 * half)) < half
        partner = jnp.where(first, pltpu.roll(x, LANES - half, 1), pltpu.roll(x, half, 1))
    return x * cos_t + partner * sin_t


def _conv_kernel(x_ref, w_ref, o_ref, pad_ref, *, seq, width, n_norm, n_scaled, scale):
    c = pl.program_id(1)
    half = (width - 1) // 2
    pad = 8
    pad_ref[0:pad, :] = jnp.zeros((pad, LANES), F32)
    pad_ref[pad + seq:pad + seq + pad, :] = jnp.zeros((pad, LANES), F32)
    pad_ref[pad:pad + seq, :] = x_ref[0]
    acc = jnp.zeros((seq, LANES), F32)
    for j in range(width):
        acc += w_ref[j:j + 1, :] * pad_ref[pl.ds(pad + j - half, seq), :]
    y = acc * jax.nn.sigmoid(acc)
    nrm = y * lax.rsqrt(jnp.sum(y * y, axis=-1, keepdims=True) + 1e-6)
    nrm = nrm * jnp.where(c < n_scaled, scale, 1.0)
    o_ref[0] = jnp.where(c < n_norm, nrm, y)


def deltanet_conv(h_all, conv_w, seq):
    bsz = h_all.shape[0]
    n_ch = conv_w.shape[1]
    width = conv_w.shape[0]
    return pl.pallas_call(
        functools.partial(_conv_kernel, seq=seq, width=width, n_norm=2 * DN_HEADS, n_scaled=DN_HEADS,
                          scale=DN_DK ** -0.5),
        grid=(bsz, n_ch // LANES),
        in_specs=[
            pl.BlockSpec((1, seq, LANES), lambda b, c: (b, 0, c)),
            pl.BlockSpec((width, LANES), lambda b, c: (0, c)),
        ],
        out_specs=pl.BlockSpec((1, seq, LANES), lambda b, c: (b, 0, c)),
        out_shape=jax.ShapeDtypeStruct((bsz, seq, n_ch), F32),
        scratch_shapes=[pltpu.VMEM((seq + 16, LANES), F32)],
        compiler_params=_params("parallel", "parallel"),
        name="deltanet_conv",
    )(h_all, conv_w)


def _delta_chunks(chains):
    def hdot(a, b):
        a_hi = a.astype(BF16)
        b_hi = b.astype(BF16)
        a_lo = (a - a_hi.astype(F32)).astype(BF16)
        b_lo = (b - b_hi.astype(F32)).astype(BF16)
        return (jnp.dot(a_hi, b_hi, preferred_element_type=F32) + jnp.dot(a_hi, b_lo, preferred_element_type=F32)
                + jnp.dot(a_lo, b_hi, preferred_element_type=F32))

    dot = functools.partial(jnp.dot, preferred_element_type=F32)
    chunk = chains[0]["qc"].shape[0]
    for c in chains:
        g_row = c["neg_a"] * jax.nn.softplus(c["a_row"] + c["dt_bias"])
        beta_row = jax.nn.sigmoid(c["b_row"])
        gc_col = jnp.sum(jnp.where(c["lower"], g_row, 0.0), axis=1, keepdims=True)
        gc_row = jnp.sum(jnp.where(c["eye"], gc_col, 0.0), axis=0, keepdims=True)
        beta_col = jnp.sum(jnp.where(c["eye"], beta_row, 0.0), axis=1, keepdims=True)
        c["decay"] = jnp.where(c["lower"], jnp.exp(jnp.minimum(gc_col - gc_row, 0.0)), 0.0)
        c["kb"] = c["kc"] * beta_col
        c["vb"] = c["vc"] * beta_col
        c["egc"] = jnp.exp(gc_col)
        c["g_last"] = jnp.sum(g_row, axis=1, keepdims=True)
        c["k_dec"] = c["kc"] * jnp.exp(c["g_last"] - gc_col)
    for c in chains:
        c["kk"] = _dot_nt(c["kb"], c["kc"])
    for c in chains:
        c["qk"] = _dot_nt(c["qc"], c["kc"])
    for c in chains:
        c["xp"] = -jnp.where(c["strict"], c["kk"] * c["decay"], 0.0)
        c["tinv"] = c["eye"].astype(F32) + c["xp"]
    span = 2
    while span < chunk:
        for c in chains:
            c["xp"] = hdot(c["xp"], c["xp"])
        for c in chains:
            c["tinv"] = c["tinv"] + hdot(c["tinv"], c["xp"])
        span *= 2
    for c in chains:
        c["uw"] = hdot(c["tinv"], jnp.concatenate([c["vb"], c["kb"] * c["egc"]], axis=1))
    for c in chains:
        dv = c["vb"].shape[1]
        c["v_new"] = c["uw"][:, :dv] - dot(c["uw"][:, dv:], c["state"])
    for c in chains:
        c["out"] = dot(c["qc"] * c["egc"], c["state"])
    for c in chains:
        c["out"] = c["out"] + dot(c["qk"] * c["decay"], c["v_new"])
    for c in chains:
        c["new_state"] = c["state"] * jnp.exp(c["g_last"]) + dot(c["k_dec"].T, c["v_new"])
    return [(c["out"], c["new_state"]) for c in chains]


def _delta_kernel(p_ref, q_ref, k_ref, v_ref, gt_ref, o_ref, s_ref, *, seq, chunk, heads_per_step):
    hg = pl.program_id(1)
    n_chunks = seq // chunk
    ii = lax.broadcasted_iota(jnp.int32, (chunk, chunk), 0)
    jj = lax.broadcasted_iota(jnp.int32, (chunk, chunk), 1)
    eye = ii == jj
    masks = ((ii >= jj, ii > jj), (ii <= jj, ii < jj))
    neg_a, dt_bias = [], []
    for hh in range(heads_per_step):
        for d in range(2):
            h = hg * heads_per_step + hh
            neg_a.append(-jnp.exp(jnp.full((1, chunk), p_ref[h, 2 * d], F32)))
            dt_bias.append(p_ref[h, 2 * d + 1])
    s_ref[...] = jnp.zeros_like(s_ref)

    def body(n, carry):
        chains, where = [], []
        for hh in range(heads_per_step):
            cols = slice(hh * LANES, (hh + 1) * LANES)
            for d in range(2):
                ch = 2 * hh + d
                c = n if d == 0 else n_chunks - 1 - n
                r0 = pl.multiple_of(c * chunk, chunk)
                rows = pl.ds(r0, chunk)
                chains.append(dict(
                    qc=q_ref[0, rows, cols], kc=k_ref[0, rows, cols], vc=v_ref[0, rows, cols],
                    a_row=gt_ref[0, hh, 2 * d, pl.ds(c, 1), :], b_row=gt_ref[0, hh, 2 * d + 1, pl.ds(c, 1), :],
                    neg_a=neg_a[ch], dt_bias=dt_bias[ch], lower=masks[d][0], strict=masks[d][1], eye=eye,
                    state=s_ref[ch]))
                where.append((ch, d, rows, cols))
        for (ch, d, rows, cols), (out, new_state) in zip(where, _delta_chunks(chains)):
            s_ref[ch] = new_state
            o_ref[d, 0, rows, cols] = out
        return carry

    lax.fori_loop(0, n_chunks, body, 0)


def deltanet_scan(qkv, gates, gate_params, seq, *, heads_per_step=4):
    bsz = qkv.shape[0]
    nh = DN_HEADS
    chunk = DN_CHUNK
    n_chunks = seq // chunk
    hps = heads_per_step
    groups = nh // hps
    width = hps * LANES
    return pl.pallas_call(
        functools.partial(_delta_kernel, seq=seq, chunk=chunk, heads_per_step=hps),
        grid=(bsz, groups),
        in_specs=[
            pl.BlockSpec(memory_space=pltpu.SMEM),
            pl.BlockSpec((1, seq, width), lambda b, g: (b, 0, g)),
            pl.BlockSpec((1, seq, width), lambda b, g: (b, 0, groups + g)),
            pl.BlockSpec((1, seq, width), lambda b, g: (b, 0, 2 * groups + g)),
            pl.BlockSpec((1, hps, 4, n_chunks, chunk), lambda b, g: (b, g, 0, 0, 0)),
        ],
        out_specs=pl.BlockSpec((2, 1, seq, width), lambda b, g: (0, b, 0, g)),
        out_shape=jax.ShapeDtypeStruct((2, bsz, seq, nh * DN_DV), F32),
        scratch_shapes=[pltpu.VMEM((2 * hps, DN_DK, DN_DV), F32)],
        compiler_params=_params("parallel", "parallel"),
        name="deltanet_scan",
    )(gate_params, qkv, qkv, qkv, gates)


def _gated_norm_kernel(of_ref, ob_ref, z_ref, g_ref, o_ref):
    o = of_ref[0, 0] + ob_ref[0, 0]
    o = o * lax.rsqrt(jnp.mean(o * o, axis=-1, keepdims=True) + EPS) * g_ref[...]
    z = z_ref[0]
    o_ref[0] = o * (z * jax.nn.sigmoid(z))


def deltanet_gated_norm(o2, h_all, z_block0, norm_g, seq, *, ts=512):
    bsz = o2.shape[1]
    nh = DN_HEADS
    return pl.pallas_call(
        _gated_norm_kernel,
        grid=(bsz, seq // ts, nh),
        in_specs=[
            pl.BlockSpec((1, 1, ts, LANES), lambda b, i, h: (0, b, i, h)),
            pl.BlockSpec((1, 1, ts, LANES), lambda b, i, h: (1, b, i, h)),
            pl.BlockSpec((1, ts, LANES), lambda b, i, h: (b, i, z_block0 + h)),
            pl.BlockSpec((1, LANES), lambda b, i, h: (0, 0)),
        ],
        out_specs=pl.BlockSpec((1, ts, LANES), lambda b, i, h: (b, i, h)),
        out_shape=jax.ShapeDtypeStruct((bsz, seq, nh * DN_DV), F32),
        compiler_params=_params("parallel", "parallel", "parallel"),
        name="deltanet_gated_norm",
    )(o2, o2, h_all, norm_g.reshape(1, DN_DV))


def _mla_attn_kernel(qn_ref, qpe_ref, kn_ref, kpe_ref, v_ref, cq_ref, sq_ref, ck_ref, sk_ref, o_ref, *, scale):
    h = pl.program_id(1)
    half = MLA_ROPE // 2
    qpe = _rope(qpe_ref[0], cq_ref[...], sq_ref[...], half)
    lane = lax.broadcasted_iota(jnp.int32, qpe.shape, 1)
    qpe = jnp.where((lane // MLA_ROPE) == (h % 2), qpe, 0.0)
    kpe = _rope(kpe_ref[0], ck_ref[...], sk_ref[...], half)
    s = _dot_nt(qn_ref[0].astype(BF16), kn_ref[0].astype(BF16))
    s += _dot_nt(qpe.astype(BF16), kpe.astype(BF16))
    s = s * scale
    m = jnp.max(s, axis=-1, keepdims=True)
    p = jnp.exp(s - m)
    den = jnp.sum(p, axis=-1, keepdims=True)
    o = jnp.dot(p.astype(BF16), v_ref[0].astype(BF16), preferred_element_type=F32)
    o_ref[0] = o / den


def mla_attention(q_up, kv_up, h_all, kpe_block, cos_t, sin_t, seq, *, tq=1024):
    bsz = q_up.shape[0]
    nh = MLA_HEADS
    scale = (MLA_NOPE + MLA_ROPE) ** -0.5
    return pl.pallas_call(
        functools.partial(_mla_attn_kernel, scale=scale),
        grid=(bsz, nh, seq // tq),
        in_specs=[
            pl.BlockSpec((1, tq, LANES), lambda b, h, i: (b, i, h)),
            pl.BlockSpec((1, tq, LANES), lambda b, h, i: (b, i, nh + h // 2)),
            pl.BlockSpec((1, seq, LANES), lambda b, h, i: (b, 0, h)),
            pl.BlockSpec((1, seq, LANES), lambda b, h, i: (b, 0, kpe_block)),
            pl.BlockSpec((1, seq, LANES), lambda b, h, i: (b, 0, nh + h)),
            pl.BlockSpec((tq, LANES), lambda b, h, i: (i, 0)),
            pl.BlockSpec((tq, LANES), lambda b, h, i: (i, 0)),
            pl.BlockSpec((seq, LANES), lambda b, h, i: (0, 0)),
            pl.BlockSpec((seq, LANES), lambda b, h, i: (0, 0)),
        ],
        out_specs=pl.BlockSpec((1, tq, LANES), lambda b, h, i: (b, i, h)),
        out_shape=jax.ShapeDtypeStruct((bsz, seq, nh * MLA_V), F32),
        compiler_params=_params("parallel", "parallel", "parallel"),
        name="mla_attention",
    )(q_up, q_up, kv_up, h_all, kv_up, cos_t, sin_t, cos_t, sin_t)


def _band_attention_multi(qs, ks, vs, q0, k0, radius, scale):
    ss = [_dot_nt(q.astype(BF16), k.astype(BF16)) * scale for q, k in zip(qs, ks)]
    qi = q0 + lax.broadcasted_iota(jnp.int32, ss[0].shape, 0)
    kj = k0 + lax.broadcasted_iota(jnp.int32, ss[0].shape, 1)
    band = jnp.abs(qi - kj) <= radius
    ss = [jnp.where(band, s, -jnp.inf) for s in ss]
    ms = [jnp.max(s, axis=-1, keepdims=True) for s in ss]
    ps = [jnp.exp(s - m) for s, m in zip(ss, ms)]
    dens = [jnp.sum(p, axis=-1, keepdims=True) for p in ps]
    os_ = [jnp.dot(p.astype(BF16), v.astype(BF16), preferred_element_type=F32) for p, v in zip(ps, vs)]
    return [(o / den, m + jnp.log(den)) for o, den, m in zip(os_, dens, ms)]


def _dil_attn_kernel(q_ref, k_ref, v_ref, cos_ref, sin_ref, o_ref, lse_ref, kr_ref, *, seq, dilation, radius, scale,
                     par):
    half = HEAD_DIM // 2
    length = seq // dilation
    tq = min(length, 256)
    tk = min(length, tq + 4 * radius)

    def load(r):
        rows = pl.ds(r, length, stride=dilation) if dilation > 1 else pl.ds(0, length)
        cos, sin = cos_ref[rows, :], sin_ref[rows, :]
        return rows, _rope(q_ref[0, rows, :], cos, sin, half), _rope(k_ref[0, rows, :], cos, sin, half)

    if tk == length:
        def body(g, carry):
            loaded = [load(g * par + i) for i in range(par)]
            vs = [v_ref[0, rows, :] for rows, _, _ in loaded]
            res = _band_attention_multi([q for _, q, _ in loaded], [k for _, _, k in loaded], vs, 0, 0, radius, scale)
            for (rows, _, _), (o, lse) in zip(loaded, res):
                o_ref[0, rows, :] = o
                lse_ref[0, rows, :] = jnp.broadcast_to(lse, o.shape)
            return carry
        lax.fori_loop(0, dilation // par, body, 0)
    else:
        _, q, k = load(0)
        kr_ref[...] = k
        for i in range(length // tq):
            k0 = min(max(i * tq - 2 * radius, 0), length - tk)
            (o, lse), = _band_attention_multi([q[i * tq:(i + 1) * tq]], [kr_ref[k0:k0 + tk, :]],
                                              [v_ref[0, k0:k0 + tk, :]], i * tq, k0, radius, scale)
            o_ref[0, i * tq:(i + 1) * tq, :] = o
            lse_ref[0, i * tq:(i + 1) * tq, :] = jnp.broadcast_to(lse, o.shape)


def dilated_group_attention(h3, group, dilation, radius, cos_t, sin_t, seq):
    bsz = h3.shape[0]
    nh = DIL_HEADS
    base = group * 3 * nh
    out_sds = jax.ShapeDtypeStruct((bsz, seq, nh * HEAD_DIM), F32)
    blk = lambda off: pl.BlockSpec((1, seq, LANES), lambda b, h: (b, 0, base + off + h))
    tbl = pl.BlockSpec((seq, LANES), lambda b, h: (0, 0))
    out = pl.BlockSpec((1, seq, LANES), lambda b, h: (b, 0, h))
    o, lse = pl.pallas_call(
        functools.partial(_dil_attn_kernel, seq=seq, dilation=dilation, radius=radius, scale=HEAD_DIM ** -0.5,
                          par=min(dilation, 4)),
        grid=(bsz, nh),
        in_specs=[blk(0), blk(nh), blk(2 * nh), tbl, tbl],
        out_specs=[out, out],
        out_shape=[out_sds, out_sds],
        scratch_shapes=[pltpu.VMEM((seq // dilation, LANES), F32)],
        compiler_params=_params("parallel", "parallel"),
        name=f"dilated_attention_g{group}",
    )(h3, h3, h3, cos_t, sin_t)
    return o.reshape(bsz * seq, nh * HEAD_DIM), lse.reshape(bsz * seq, nh * HEAD_DIM)


def _dil_combine_kernel(*refs):
    o_ref = refs[-1]
    n = (len(refs) - 1) // 2
    outs = [refs[2 * g][...] for g in range(n)]
    lses = [refs[2 * g + 1][...] for g in range(n)]
    m = functools.reduce(jnp.maximum, lses)
    es = [jnp.exp(l - m) for l in lses]
    num = functools.reduce(jnp.add, [e * o for e, o in zip(es, outs)])
    o_ref[...] = num / functools.reduce(jnp.add, es)


def dilated_combine(pairs, *, tm=256):
    m, d = pairs[0][0].shape
    flat = [a for pair in pairs for a in pair]
    spec = pl.BlockSpec((tm, d), lambda i: (i, 0))
    return pl.pallas_call(
        _dil_combine_kernel,
        grid=(m // tm,),
        in_specs=[spec] * len(flat),
        out_specs=spec,
        out_shape=jax.ShapeDtypeStruct((m, d), F32),
        compiler_params=_params("parallel"),
        name="dilated_combine",
    )(*flat)


def _top_rows(s, k, payload=None):
    rows = s.shape[0]
    idx = lax.broadcasted_iota(jnp.int32, s.shape, 0).astype(F32)
    vals, inds, pays = [], [], []
    for _ in range(k):
        m = jnp.max(s, axis=0, keepdims=True)
        sel = jnp.min(jnp.where(s == m, idx, float(rows)), axis=0, keepdims=True)
        hit = idx == sel
        vals.append(m)
        inds.append(sel)
        if payload is not None:
            pays.append(jnp.sum(jnp.where(hit, payload, 0.0), axis=0, keepdims=True))
        s = jnp.where(hit, -jnp.inf, s)
    cat = lambda xs: jnp.concatenate(xs, axis=0)
    return cat(vals), cat(inds), (cat(pays) if payload is not None else None)


def _peer_select_kernel(qt_ref, keys_ref, e_ref, g_ref):
    kk = PEER_TOPK
    half = PEER_QUERY_DIM // 2
    for lb in range(qt_ref.shape[1] // LANES):
        lanes = slice(lb * LANES, (lb + 1) * LANES)
        tops = []
        for p in range(2):
            s = jnp.dot(keys_ref[p], qt_ref[p * half:(p + 1) * half, lanes], preferred_element_type=F32)
            tops.append(_top_rows(s, kk)[:2])
        (s0, i0), (s1, i1) = tops
        sub = lax.broadcasted_iota(jnp.int32, (8, LANES), 0)
        pieces_s, pieces_e = [], []
        for a in range(kk // 2):
            nb = kk // (a + 1)
            rows = kk if nb > 8 else 8
            ps = s0[a:a + 1] + s1[0:rows]
            if nb < rows:
                ps = jnp.where(sub < nb, ps, -jnp.inf)
            pieces_s.append(ps)
            pieces_e.append(i0[a:a + 1] * float(PEER_N_KEYS) + i1[0:rows])
        pieces_s.append(s0[kk // 2:kk] + s1[0:1])
        pieces_e.append(i0[kk // 2:kk] * float(PEER_N_KEYS) + i1[0:1])
        cand_s = jnp.concatenate(pieces_s, axis=0)
        cand_e = jnp.concatenate(pieces_e, axis=0)
        best_s, _, experts = _top_rows(cand_s, kk, payload=cand_e)
        ex = jnp.exp(best_s - best_s[0:1])
        e_ref[0, :, lanes] = experts.astype(jnp.int32)
        g_ref[0, :, lanes] = ex / jnp.sum(ex, axis=0, keepdims=True)


def peer_select(q_t, sub_keys, *, tt=512):
    t = q_t.shape[1]
    nh = PEER_HEADS
    kk = PEER_TOPK
    return pl.pallas_call(
        _peer_select_kernel,
        grid=(t // tt, nh),
        in_specs=[
            pl.BlockSpec((PEER_QUERY_DIM, tt), lambda i, h: (h, i)),
            pl.BlockSpec((2, PEER_N_KEYS, PEER_QUERY_DIM // 2), lambda i, h: (0, 0, 0)),
        ],
        out_specs=[
            pl.BlockSpec((1, kk, tt), lambda i, h: (h, 0, i)),
            pl.BlockSpec((1, kk, tt), lambda i, h: (h, 0, i)),
        ],
        out_shape=[jax.ShapeDtypeStruct((nh, kk, t), jnp.int32), jax.ShapeDtypeStruct((nh, kk, t), F32)],
        compiler_params=_params("parallel", "parallel"),
        name="peer_select",
    )(q_t, sub_keys)


def _erf(x):
    x = jnp.clip(x, -4.0, 4.0)
    x2 = x * x
    alpha = (-2.72614225801306e-10, 2.77068142495902e-08, -2.10102402082508e-06, -5.69250639462346e-05,
             -7.34990630326855e-04, -2.95459980854025e-03, -1.60960333262415e-02)
    beta = (-1.45660718464996e-05, -2.13374055278905e-04, -1.68282697438203e-03, -7.37332916720468e-03,
            -1.42647390514189e-02)
    p = jnp.full_like(x, alpha[0])
    for c in alpha[1:]:
        p = p * x2 + c
    q = jnp.full_like(x, beta[0])
    for c in beta[1:]:
        q = q * x2 + c
    return x * p / q


PEER_SLOTS = 8


def _peer_expert_kernel(idx_hbm, g_ref, x_ref, uv_hbm, lng_ref, lnb_ref, y_ref, idx_smem, buf, isem, sem, *, tb, d, alpha):
    n_sel = PEER_SEL
    n_chunks = d // LANES
    ns = PEER_SLOTS
    step = pl.program_id(0)
    idx_copy = pltpu.make_async_copy(idx_hbm.at[pl.ds(step * (tb * n_sel), tb * n_sel)], idx_smem, isem)
    idx_copy.start()
    idx_copy.wait()

    def row_copy(slot, row, j):
        return pltpu.make_async_copy(uv_hbm.at[row], buf.at[slot, :, j, :], sem.at[slot])

    def issue(t, slot):
        base = t * n_sel
        for j in range(n_sel):
            row_copy(slot, idx_smem[base + j], j).start(priority=j % 2)

    def wait(slot):
        for j in range(n_sel):
            row_copy(slot, 0, j).wait()

    ii = lax.broadcasted_iota(jnp.int32, (n_sel, n_sel), 0)
    jj = lax.broadcasted_iota(jnp.int32, (n_sel, n_sel), 1)
    eye = ii == jj
    high = jnp.uint32(0xFFFF0000)

    def consume(t, slot):
        x_row = x_ref[pl.ds(t, 1), :]
        acc = jnp.zeros((n_sel, LANES), F32)
        for c in range(n_chunks):
            u_c = lax.bitcast_convert_type(buf[slot, c] & high, F32)
            acc += u_c * x_row[:, c * LANES:(c + 1) * LANES]
        hid = jnp.sum(acc, axis=1, keepdims=True)
        g_col = jnp.sum(jnp.where(eye, g_ref[pl.ds(t, 1), :], 0.0), axis=1, keepdims=True)
        act = g_col * (0.5 * hid * (1.0 + _erf(hid * (2.0 ** -0.5))))
        y_chunks = []
        for c in range(n_chunks):
            v_c = lax.bitcast_convert_type(buf[slot, c] << 16, F32)
            y_chunks.append(jnp.sum(v_c * act, axis=0, keepdims=True))
        y_ref[pl.ds(t, 1), :] = jnp.concatenate(y_chunks, axis=1)

    for t0 in range(ns - 1):
        issue(t0, t0)

    def group(k, carry):
        for j in range(ns):
            t = k * ns + j
            issue(t + ns - 1, (j + ns - 1) % ns)
            wait(j)
            consume(t, j)
        return carry

    lax.fori_loop(0, tb // ns - 1, group, 0)
    for j in range(ns):
        t = tb - ns + j
        if j == 0:
            issue(tb - 1, ns - 1)
        wait(j)
        consume(t, j)
    z = alpha * x_ref[...] + y_ref[...]
    mu = jnp.mean(z, axis=-1, keepdims=True)
    zc = z - mu
    var = jnp.mean(zc * zc, axis=-1, keepdims=True)
    y_ref[...] = zc * lax.rsqrt(var + EPS) * lng_ref[...] + lnb_ref[...]


def peer_experts(idx, gates, x, uv, ln_g, ln_b, alpha, *, tb=512):
    t, d = x.shape
    n_sel = PEER_SEL
    return pl.pallas_call(
        functools.partial(_peer_expert_kernel, tb=tb, d=d, alpha=alpha),
        grid=(t // tb,),
        in_specs=[
            pl.BlockSpec(memory_space=pl.ANY),
            pl.BlockSpec((tb, n_sel), lambda i: (i, 0)),
            pl.BlockSpec((tb, d), lambda i: (i, 0)),
            pl.BlockSpec(memory_space=pl.ANY),
            pl.BlockSpec((1, d), lambda i: (0, 0)),
            pl.BlockSpec((1, d), lambda i: (0, 0)),
        ],
        out_specs=pl.BlockSpec((tb, d), lambda i: (i, 0)),
        out_shape=jax.ShapeDtypeStruct((t, d), F32),
        scratch_shapes=[
            pltpu.SMEM((tb * n_sel,), jnp.int32),
            pltpu.VMEM((PEER_SLOTS, d // LANES, n_sel, LANES), jnp.uint32),
            pltpu.SemaphoreType.DMA,
            pltpu.SemaphoreType.DMA((PEER_SLOTS,)),
        ],
        compiler_params=_params("arbitrary"),
        name="peer_experts",
    )(idx, gates, x, uv, ln_g.reshape(1, d), ln_b.reshape(1, d))


def _pack_expert_tables(u, v):
    e, d = u.shape
    hi = lax.bitcast_convert_type(u.astype(BF16), jnp.uint16).astype(jnp.uint32) << 16
    lo = lax.bitcast_convert_type(v.astype(BF16), jnp.uint16).astype(jnp.uint32)
    return (hi | lo).reshape(e, d // LANES, LANES)


def peer_layer(x, w_q, sub_keys, u, v, ln_g, ln_b, alpha):
    t = x.shape[0]
    q_t = matmul_nt(w_q.T.astype(BF16), x)
    experts, gates = peer_select(q_t, sub_keys)
    idx = experts.reshape(PEER_SEL, t).T.reshape(t * PEER_SEL)
    gates = gates.reshape(PEER_SEL, t).T
    return peer_experts(idx, gates, x, _pack_expert_tables(u, v), ln_g, ln_b, alpha)


def _even_mixer(xt, bsz, seq, w_in, conv_w, gate_params, norm_g, q_norm_g, w_uq, kv_norm_g, w_ukv, w_out,
                cos_m, sin_m):
    nh = DN_HEADS
    c_qkvz = 4 * nh * DN_DK
    c_gate = c_qkvz + 4 * nh
    c_cq = c_gate + MLA_Q_RANK
    c_ckv = c_cq + MLA_KV_RANK
    w_gate = w_in[:, c_qkvz:c_gate]
    w_kr = w_in[:, c_ckv:]
    w_all = jnp.concatenate(
        [w_in[:, :c_qkvz], w_in[:, c_gate:c_ckv], w_kr, w_kr, w_gate,
         jnp.zeros((w_in.shape[0], LANES - 4 * nh), w_in.dtype)], axis=1).astype(BF16)
    h_all = matmul(xt, w_all, tn=768)
    cols = w_all.shape[1]
    h3d = h_all.reshape(bsz, seq, cols)
    lat0 = c_qkvz
    kpe_block = (lat0 + MLA_Q_RANK + MLA_KV_RANK) // LANES
    g0 = lat0 + MLA_Q_RANK + MLA_KV_RANK + LANES

    qkv = deltanet_conv(h3d, conv_w, seq)
    n_chunks = seq // DN_CHUNK
    gates = h_all[:, g0:g0 + 4 * nh].reshape(bsz, n_chunks, DN_CHUNK, 4, nh).transpose(0, 4, 3, 1, 2)
    o2 = deltanet_scan(qkv, gates, gate_params, seq)
    o_a = deltanet_gated_norm(o2, h3d, 3 * nh, norm_g, seq)

    hq = MLA_NOPE + MLA_ROPE
    w_uq_r = w_uq.reshape(MLA_Q_RANK, MLA_HEADS, hq)
    w_uq_p = jnp.concatenate([w_uq_r[:, :, :MLA_NOPE].reshape(MLA_Q_RANK, -1),
                              w_uq_r[:, :, MLA_NOPE:].reshape(MLA_Q_RANK, -1)], axis=1).astype(BF16)
    w_ukv_r = w_ukv.reshape(MLA_KV_RANK, MLA_HEADS, MLA_NOPE + MLA_V)
    w_ukv_p = jnp.concatenate([w_ukv_r[:, :, :MLA_NOPE].reshape(MLA_KV_RANK, -1),
                               w_ukv_r[:, :, MLA_NOPE:].reshape(MLA_KV_RANK, -1)], axis=1).astype(BF16)
    q_up = rms_matmul(h_all, lat0 // MLA_Q_RANK, q_norm_g, w_uq_p)
    kv_up = rms_matmul(h_all, (lat0 + MLA_Q_RANK) // MLA_KV_RANK, kv_norm_g, w_ukv_p)
    o_b = mla_attention(q_up.reshape(bsz, seq, -1), kv_up.reshape(bsz, seq, -1), h3d, kpe_block, cos_m, sin_m, seq)

    n_a = nh * DN_DV
    w_out = w_out.astype(BF16)
    return [(o_a.reshape(bsz * seq, n_a), w_out[:n_a]), (o_b.reshape(bsz * seq, -1), w_out[n_a:])]


def _odd_mixer(xt, bsz, seq, w_in, w_out, cos_h, sin_h):
    h3 = matmul(xt, w_in.astype(BF16), tn=768).reshape(bsz, seq, -1)
    pairs = []
    for g, (window, dilation) in enumerate(DIL_GROUPS):
        pairs.append(dilated_group_attention(h3, g, dilation, window // (2 * dilation), cos_h, sin_h, seq))
    return [(dilated_combine(pairs), w_out.astype(BF16))]


def kernel(x, ab_w_in, a_conv_w, a_log_f, a_dt_bias_f, a_log_b, a_dt_bias_b, a_out_norm_g, b_q_norm_g, b_w_uq,
           b_kv_norm_g, b_w_ukv, ab_w_out, c_w_in, c_w_out, mix_ln_g, mix_ln_b, peer_w_q, peer_sub_keys, peer_u,
           peer_v, ffn_ln_g, ffn_ln_b):
    bsz, seq, d = x.shape
    depth = mix_ln_g.shape[0]
    alpha = (2 * depth) ** 0.25
    cos_h, sin_h = _rope_tables(seq, HEAD_DIM)
    cos_m, sin_m = _rope_tables(seq, MLA_ROPE)
    xt = x.reshape(bsz * seq, d)
    for layer in range(depth):
        i = layer // 2
        if layer % 2 == 0:
            gate_params = jnp.stack([a_log_f[i], a_dt_bias_f[i], a_log_b[i], a_dt_bias_b[i]], axis=1)
            mix = _even_mixer(xt, bsz, seq, ab_w_in[i], a_conv_w[i], gate_params, a_out_norm_g[i], b_q_norm_g[i],
                              b_w_uq[i], b_kv_norm_g[i], b_w_ukv[i], ab_w_out[i], cos_m, sin_m)
        else:
            mix = _odd_mixer(xt, bsz, seq, c_w_in[i], c_w_out[i], cos_h, sin_h)
        xt = matmul_add_layer_norm(mix, xt, mix_ln_g[layer], mix_ln_b[layer], alpha)
        xt = peer_layer(xt, peer_w_q[layer], peer_sub_keys[layer], peer_u[layer], peer_v[layer],
                        ffn_ln_g[layer], ffn_ln_b[layer], alpha)
    return xt.reshape(bsz, seq, d)
```

```python
import functools

import jax
import jax.numpy as jnp
from jax import lax
from jax.experimental import pallas as pl
from jax.experimental.pallas import tpu as pltpu

F32 = jnp.float32
BF16 = jnp.bfloat16

HEAD_DIM = 128
ROPE_THETA = 10000.0
EPS = 1e-5
LANES = 128

DN_HEADS = 8
DN_DK = 128
DN_DV = 128
DN_CONV = 5
DN_CHUNK = 64

MLA_HEADS = 8
MLA_Q_RANK = 512
MLA_KV_RANK = 512
MLA_NOPE = 128
MLA_ROPE = 64
MLA_V = 128

DIL_GROUPS = ((128, 1), (512, 4), (2048, 16))
DIL_HEADS = 8

PEER_HEADS = 8
PEER_N_KEYS = 128
PEER_QUERY_DIM = 256
PEER_TOPK = 16
PEER_SEL = PEER_HEADS * PEER_TOPK

VMEM_LIMIT = 48 * 1024 * 1024


def _params(*sem):
    return pltpu.CompilerParams(dimension_semantics=sem, vmem_limit_bytes=VMEM_LIMIT)


def _dot_nt(a, b, **kw):
    return lax.dot_general(a, b, (((1,), (1,)), ((), ())), preferred_element_type=F32, **kw)


def _mm_kernel(a_ref, b_ref, o_ref):
    o_ref[...] = jnp.dot(a_ref[...].astype(BF16), b_ref[...], preferred_element_type=F32).astype(o_ref.dtype)


def matmul(a, b, *, tm=1024, tn=512, out_dtype=F32):
    m, k = a.shape
    n = b.shape[1]
    return pl.pallas_call(
        _mm_kernel,
        grid=(n // tn, m // tm),
        in_specs=[pl.BlockSpec((tm, k), lambda j, i: (i, 0)), pl.BlockSpec((k, tn), lambda j, i: (0, j))],
        out_specs=pl.BlockSpec((tm, tn), lambda j, i: (i, j)),
        out_shape=jax.ShapeDtypeStruct((m, n), out_dtype),
        compiler_params=_params("parallel", "parallel"),
        name="matmul",
    )(a, b)


def _mm_ln_kernel(*refs, n_pairs, alpha):
    x_ref, g_ref, b_ref, o_ref = refs[2 * n_pairs:]
    acc = alpha * x_ref[...]
    for p in range(n_pairs):
        acc += jnp.dot(refs[2 * p][...].astype(BF16), refs[2 * p + 1][...], preferred_element_type=F32)
    mu = jnp.mean(acc, axis=-1, keepdims=True)
    zc = acc - mu
    var = jnp.mean(zc * zc, axis=-1, keepdims=True)
    o_ref[...] = zc * lax.rsqrt(var + EPS) * g_ref[...] + b_ref[...]


def matmul_add_layer_norm(pairs, x, g, b, alpha, *, tm=512):
    m, d = x.shape
    flat, specs = [], []
    for a, w in pairs:
        k = a.shape[1]
        flat += [a, w]
        specs += [pl.BlockSpec((tm, k), lambda i: (i, 0)), pl.BlockSpec((k, d), lambda i: (0, 0))]
    row = pl.BlockSpec((tm, d), lambda i: (i, 0))
    vec = pl.BlockSpec((1, d), lambda i: (0, 0))
    return pl.pallas_call(
        functools.partial(_mm_ln_kernel, n_pairs=len(pairs), alpha=alpha),
        grid=(m // tm,),
        in_specs=specs + [row, vec, vec],
        out_specs=row,
        out_shape=jax.ShapeDtypeStruct((m, d), F32),
        compiler_params=_params("parallel"),
        name="matmul_add_layer_norm",
    )(*flat, x, g.reshape(1, d), b.reshape(1, d))


def _mm_nt_kernel(a_ref, b_ref, o_ref):
    o_ref[...] = _dot_nt(a_ref[...], b_ref[...].astype(BF16))


def matmul_nt(a, b, *, tm=2048, tn=512):
    m, k = a.shape
    n = b.shape[0]
    return pl.pallas_call(
        _mm_nt_kernel,
        grid=(m // tm, n // tn),
        in_specs=[pl.BlockSpec((tm, k), lambda i, j: (i, 0)), pl.BlockSpec((tn, k), lambda i, j: (j, 0))],
        out_specs=pl.BlockSpec((tm, tn), lambda i, j: (i, j)),
        out_shape=jax.ShapeDtypeStruct((m, n), F32),
        compiler_params=_params("parallel", "parallel"),
        name="matmul_nt",
    )(a, b)


def _rms_mm_kernel(a_ref, g_ref, b_ref, o_ref):
    a = a_ref[...]
    a = a * lax.rsqrt(jnp.mean(a * a, axis=-1, keepdims=True) + EPS) * g_ref[...]
    o_ref[...] = jnp.dot(a.astype(BF16), b_ref[...], preferred_element_type=F32)


def rms_matmul(a, col_block, g, b, *, tm=1024, tn=512):
    m = a.shape[0]
    k, n = b.shape
    return pl.pallas_call(
        _rms_mm_kernel,
        grid=(n // tn, m // tm),
        in_specs=[
            pl.BlockSpec((tm, k), lambda j, i: (i, col_block)),
            pl.BlockSpec((1, k), lambda j, i: (0, 0)),
            pl.BlockSpec((k, tn), lambda j, i: (0, j)),
        ],
        out_specs=pl.BlockSpec((tm, tn), lambda j, i: (i, j)),
        out_shape=jax.ShapeDtypeStruct((m, n), F32),
        compiler_params=_params("parallel", "parallel"),
        name="rms_matmul",
    )(a, g.reshape(1, k), b)


def _rope_tables(seq, dim):
    half = dim // 2
    inv_freq = ROPE_THETA ** (-jnp.arange(0, dim, 2, dtype=F32) / dim)
    ang = jnp.arange(seq, dtype=F32)[:, None] * inv_freq[None, :]
    cos, sin = jnp.cos(ang), jnp.sin(ang)
    reps = LANES // dim
    cos_t = jnp.tile(jnp.concatenate([cos, cos], axis=-1), (1, reps))
    sin_t = jnp.tile(jnp.concatenate([-sin, sin], axis=-1), (1, reps))
    return cos_t, sin_t


def _rope(x, cos_t, sin_t, half):
    if 2 * half == LANES:
        partner = pltpu.roll(x, half, 1)
    else:
        lane = lax.broadcasted_iota(jnp.int32, x.shape, 1)
        first = (lane % (2 * half)) < half
        partner = jnp.where(first, pltpu.roll(x, LANES - half, 1), pltpu.roll(x, half, 1))
    return x * cos_t + partner * sin_t


def _conv_kernel(x_ref, w_ref, o_ref, pad_ref, *, seq, width, n_norm, n_scaled, scale):
    c = pl.program_id(1)
    half = (width - 1) // 2
    pad = 8
    pad_ref[0:pad, :] = jnp.zeros((pad, LANES), F32)
    pad_ref[pad + seq:pad + seq + pad, :] = jnp.zeros((pad, LANES), F32)
    pad_ref[pad:pad + seq, :] = x_ref[0]
    acc = jnp.zeros((seq, LANES), F32)
    for j in range(width):
        acc += w_ref[j:j + 1, :] * pad_ref[pl.ds(pad + j - half, seq), :]
    y = acc * jax.nn.sigmoid(acc)
    nrm = y * lax.rsqrt(jnp.sum(y * y, axis=-1, keepdims=True) + 1e-6)
    nrm = nrm * jnp.where(c < n_scaled, scale, 1.0)
    o_ref[0] = jnp.where(c < n_norm, nrm, y)


def deltanet_conv(h_all, conv_w, seq):
    bsz = h_all.shape[0]
    n_ch = conv_w.shape[1]
    width = conv_w.shape[0]
    return pl.pallas_call(
        functools.partial(_conv_kernel, seq=seq, width=width, n_norm=2 * DN_HEADS, n_scaled=DN_HEADS,
                          scale=DN_DK ** -0.5),
        grid=(bsz, n_ch // LANES),
        in_specs=[
            pl.BlockSpec((1, seq, LANES), lambda b, c: (b, 0, c)),
            pl.BlockSpec((width, LANES), lambda b, c: (0, c)),
        ],
        out_specs=pl.BlockSpec((1, seq, LANES), lambda b, c: (b, 0, c)),
        out_shape=jax.ShapeDtypeStruct((bsz, seq, n_ch), F32),
        scratch_shapes=[pltpu.VMEM((seq + 16, LANES), F32)],
        compiler_params=_params("parallel", "parallel"),
        name="deltanet_conv",
    )(h_all, conv_w)


def _delta_chunks(chains):
    def hdot(a, b):
        a_hi = a.astype(BF16)
        b_hi = b.astype(BF16)
        a_lo = (a - a_hi.astype(F32)).astype(BF16)
        b_lo = (b - b_hi.astype(F32)).astype(BF16)
        return (jnp.dot(a_hi, b_hi, preferred_element_type=F32) + jnp.dot(a_hi, b_lo, preferred_element_type=F32)
                + jnp.dot(a_lo, b_hi, preferred_element_type=F32))

    dot = functools.partial(jnp.dot, preferred_element_type=F32)
    chunk = chains[0]["qc"].shape[0]
    for c in chains:
        g_row = c["neg_a"] * jax.nn.softplus(c["a_row"] + c["dt_bias"])
        beta_row = jax.nn.sigmoid(c["b_row"])
        gc_col = jnp.sum(jnp.where(c["lower"], g_row, 0.0), axis=1, keepdims=True)
        gc_row = jnp.sum(jnp.where(c["eye"], gc_col, 0.0), axis=0, keepdims=True)
        beta_col = jnp.sum(jnp.where(c["eye"], beta_row, 0.0), axis=1, keepdims=True)
        c["decay"] = jnp.where(c["lower"], jnp.exp(jnp.minimum(gc_col - gc_row, 0.0)), 0.0)
        c["kb"] = c["kc"] * beta_col
        c["vb"] = c["vc"] * beta_col
        c["egc"] = jnp.exp(gc_col)
        c["g_last"] = jnp.sum(g_row, axis=1, keepdims=True)
        c["k_dec"] = c["kc"] * jnp.exp(c["g_last"] - gc_col)
    for c in chains:
        c["kk"] = _dot_nt(c["kb"], c["kc"])
    for c in chains:
        c["qk"] = _dot_nt(c["qc"], c["kc"])
    for c in chains:
        c["xp"] = -jnp.where(c["strict"], c["kk"] * c["decay"], 0.0)
        c["tinv"] = c["eye"].astype(F32) + c["xp"]
    span = 2
    while span < chunk:
        for c in chains:
            c["xp"] = hdot(c["xp"], c["xp"])
        for c in chains:
            c["tinv"] = c["tinv"] + hdot(c["tinv"], c["xp"])
        span *= 2
    for c in chains:
        c["uw"] = hdot(c["tinv"], jnp.concatenate([c["vb"], c["kb"] * c["egc"]], axis=1))
    for c in chains:
        dv = c["vb"].shape[1]
        c["v_new"] = c["uw"][:, :dv] - dot(c["uw"][:, dv:], c["state"])
    for c in chains:
        c["out"] = dot(c["qc"] * c["egc"], c["state"])
    for c in chains:
        c["out"] = c["out"] + dot(c["qk"] * c["decay"], c["v_new"])
    for c in chains:
        c["new_state"] = c["state"] * jnp.exp(c["g_last"]) + dot(c["k_dec"].T, c["v_new"])
    return [(c["out"], c["new_state"]) for c in chains]


def _delta_kernel(p_ref, q_ref, k_ref, v_ref, gt_ref, o_ref, s_ref, *, seq, chunk, heads_per_step):
    hg = pl.program_id(1)
    n_chunks = seq // chunk
    ii = lax.broadcasted_iota(jnp.int32, (chunk, chunk), 0)
    jj = lax.broadcasted_iota(jnp.int32, (chunk, chunk), 1)
    eye = ii == jj
    masks = ((ii >= jj, ii > jj), (ii <= jj, ii < jj))
    neg_a, dt_bias = [], []
    for hh in range(heads_per_step):
        for d in range(2):
            h = hg * heads_per_step + hh
            neg_a.append(-jnp.exp(jnp.full((1, chunk), p_ref[h, 2 * d], F32)))
            dt_bias.append(p_ref[h, 2 * d + 1])
    s_ref[...] = jnp.zeros_like(s_ref)

    def body(n, carry):
        chains, where = [], []
        for hh in range(heads_per_step):
            cols = slice(hh * LANES, (hh + 1) * LANES)
            for d in range(2):
                ch = 2 * hh + d
                c = n if d == 0 else n_chunks - 1 - n
                r0 = pl.multiple_of(c * chunk, chunk)
                rows = pl.ds(r0, chunk)
                chains.append(dict(
                    qc=q_ref[0, rows, cols], kc=k_ref[0, rows, cols], vc=v_ref[0, rows, cols],
                    a_row=gt_ref[0, hh, 2 * d, pl.ds(c, 1), :], b_row=gt_ref[0, hh, 2 * d + 1, pl.ds(c, 1), :],
                    neg_a=neg_a[ch], dt_bias=dt_bias[ch], lower=masks[d][0], strict=masks[d][1], eye=eye,
                    state=s_ref[ch]))
                where.append((ch, d, rows, cols))
        for (ch, d, rows, cols), (out, new_state) in zip(where, _delta_chunks(chains)):
            s_ref[ch] = new_state
            o_ref[d, 0, rows, cols] = out
        return carry

    lax.fori_loop(0, n_chunks, body, 0)


def deltanet_scan(qkv, gates, gate_params, seq, *, heads_per_step=4):
    bsz = qkv.shape[0]
    nh = DN_HEADS
    chunk = DN_CHUNK
    n_chunks = seq // chunk
    hps = heads_per_step
    groups = nh // hps
    width = hps * LANES
    return pl.pallas_call(
        functools.partial(_delta_kernel, seq=seq, chunk=chunk, heads_per_step=hps),
        grid=(bsz, groups),
        in_specs=[
            pl.BlockSpec(memory_space=pltpu.SMEM),
            pl.BlockSpec((1, seq, width), lambda b, g: (b, 0, g)),
            pl.BlockSpec((1, seq, width), lambda b, g: (b, 0, groups + g)),
            pl.BlockSpec((1, seq, width), lambda b, g: (b, 0, 2 * groups + g)),
            pl.BlockSpec((1, hps, 4, n_chunks, chunk), lambda b, g: (b, g, 0, 0, 0)),
        ],
        out_specs=pl.BlockSpec((2, 1, seq, width), lambda b, g: (0, b, 0, g)),
        out_shape=jax.ShapeDtypeStruct((2, bsz, seq, nh * DN_DV), F32),
        scratch_shapes=[pltpu.VMEM((2 * hps, DN_DK, DN_DV), F32)],
        compiler_params=_params("parallel", "parallel"),
        name="deltanet_scan",
    )(gate_params, qkv, qkv, qkv, gates)


def _gated_norm_kernel(of_ref, ob_ref, z_ref, g_ref, o_ref):
    o = of_ref[0, 0] + ob_ref[0, 0]
    o = o * lax.rsqrt(jnp.mean(o * o, axis=-1, keepdims=True) + EPS) * g_ref[...]
    z = z_ref[0]
    o_ref[0] = o * (z * jax.nn.sigmoid(z))


def deltanet_gated_norm(o2, h_all, z_block0, norm_g, seq, *, ts=512):
    bsz = o2.shape[1]
    nh = DN_HEADS
    return pl.pallas_call(
        _gated_norm_kernel,
        grid=(bsz, seq // ts, nh),
        in_specs=[
            pl.BlockSpec((1, 1, ts, LANES), lambda b, i, h: (0, b, i, h)),
            pl.BlockSpec((1, 1, ts, LANES), lambda b, i, h: (1, b, i, h)),
            pl.BlockSpec((1, ts, LANES), lambda b, i, h: (b, i, z_block0 + h)),
            pl.BlockSpec((1, LANES), lambda b, i, h: (0, 0)),
        ],
        out_specs=pl.BlockSpec((1, ts, LANES), lambda b, i, h: (b, i, h)),
        out_shape=jax.ShapeDtypeStruct((bsz, seq, nh * DN_DV), F32),
        compiler_params=_params("parallel", "parallel", "parallel"),
        name="deltanet_gated_norm",
    )(o2, o2, h_all, norm_g.reshape(1, DN_DV))


def _mla_attn_kernel(qn_ref, qpe_ref, kn_ref, kpe_ref, v_ref, cq_ref, sq_ref, ck_ref, sk_ref, o_ref, *, scale):
    h = pl.program_id(1)
    half = MLA_ROPE // 2
    qpe = _rope(qpe_ref[0], cq_ref[...], sq_ref[...], half)
    lane = lax.broadcasted_iota(jnp.int32, qpe.shape, 1)
    qpe = jnp.where((lane // MLA_ROPE) == (h % 2), qpe, 0.0)
    kpe = _rope(kpe_ref[0], ck_ref[...], sk_ref[...], half)
    s = _dot_nt(qn_ref[0].astype(BF16), kn_ref[0].astype(BF16))
    s += _dot_nt(qpe.astype(BF16), kpe.astype(BF16))
    s = s * scale
    m = jnp.max(s, axis=-1, keepdims=True)
    p = jnp.exp(s - m)
    den = jnp.sum(p, axis=-1, keepdims=True)
    o = jnp.dot(p.astype(BF16), v_ref[0].astype(BF16), preferred_element_type=F32)
    o_ref[0] = o / den


def mla_attention(q_up, kv_up, h_all, kpe_block, cos_t, sin_t, seq, *, tq=1024):
    bsz = q_up.shape[0]
    nh = MLA_HEADS
    scale = (MLA_NOPE + MLA_ROPE) ** -0.5
    return pl.pallas_call(
        functools.partial(_mla_attn_kernel, scale=scale),
        grid=(bsz, nh, seq // tq),
        in_specs=[
            pl.BlockSpec((1, tq, LANES), lambda b, h, i: (b, i, h)),
            pl.BlockSpec((1, tq, LANES), lambda b, h, i: (b, i, nh + h // 2)),
            pl.BlockSpec((1, seq, LANES), lambda b, h, i: (b, 0, h)),
            pl.BlockSpec((1, seq, LANES), lambda b, h, i: (b, 0, kpe_block)),
            pl.BlockSpec((1, seq, LANES), lambda b, h, i: (b, 0, nh + h)),
            pl.BlockSpec((tq, LANES), lambda b, h, i: (i, 0)),
            pl.BlockSpec((tq, LANES), lambda b, h, i: (i, 0)),
            pl.BlockSpec((seq, LANES), lambda b, h, i: (0, 0)),
            pl.BlockSpec((seq, LANES), lambda b, h, i: (0, 0)),
        ],
        out_specs=pl.BlockSpec((1, tq, LANES), lambda b, h, i: (b, i, h)),
        out_shape=jax.ShapeDtypeStruct((bsz, seq, nh * MLA_V), F32),
        compiler_params=_params("parallel", "parallel", "parallel"),
        name="mla_attention",
    )(q_up, q_up, kv_up, h_all, kv_up, cos_t, sin_t, cos_t, sin_t)


def _band_attention_multi(qs, ks, vs, q0, k0, radius, scale):
    ss = [_dot_nt(q.astype(BF16), k.astype(BF16)) * scale for q, k in zip(qs, ks)]
    qi = q0 + lax.broadcasted_iota(jnp.int32, ss[0].shape, 0)
    kj = k0 + lax.broadcasted_iota(jnp.int32, ss[0].shape, 1)
    band = jnp.abs(qi - kj) <= radius
    ss = [jnp.where(band, s, -jnp.inf) for s in ss]
    ms = [jnp.max(s, axis=-1, keepdims=True) for s in ss]
    ps = [jnp.exp(s - m) for s, m in zip(ss, ms)]
    dens = [jnp.sum(p, axis=-1, keepdims=True) for p in ps]
    os_ = [jnp.dot(p.astype(BF16), v.astype(BF16), preferred_element_type=F32) for p, v in zip(ps, vs)]
    return [(o / den, m + jnp.log(den)) for o, den, m in zip(os_, dens, ms)]


def _dil_attn_kernel(q_ref, k_ref, v_ref, cos_ref, sin_ref, o_ref, lse_ref, kr_ref, *, seq, dilation, radius, scale,
                     par):
    half = HEAD_DIM // 2
    length = seq // dilation
    tq = min(length, 256)
    tk = min(length, tq + 4 * radius)

    def load(r):
        rows = pl.ds(r, length, stride=dilation) if dilation > 1 else pl.ds(0, length)
        cos, sin = cos_ref[rows, :], sin_ref[rows, :]
        return rows, _rope(q_ref[0, rows, :], cos, sin, half), _rope(k_ref[0, rows, :], cos, sin, half)

    if tk == length:
        def body(g, carry):
            loaded = [load(g * par + i) for i in range(par)]
            vs = [v_ref[0, rows, :] for rows, _, _ in loaded]
            res = _band_attention_multi([q for _, q, _ in loaded], [k for _, _, k in loaded], vs, 0, 0, radius, scale)
            for (rows, _, _), (o, lse) in zip(loaded, res):
                o_ref[0, rows, :] = o
                lse_ref[0, rows, :] = jnp.broadcast_to(lse, o.shape)
            return carry
        lax.fori_loop(0, dilation // par, body, 0)
    else:
        _, q, k = load(0)
        kr_ref[...] = k
        for i in range(length // tq):
            k0 = min(max(i * tq - 2 * radius, 0), length - tk)
            (o, lse), = _band_attention_multi([q[i * tq:(i + 1) * tq]], [kr_ref[k0:k0 + tk, :]],
                                              [v_ref[0, k0:k0 + tk, :]], i * tq, k0, radius, scale)
            o_ref[0, i * tq:(i + 1) * tq, :] = o
            lse_ref[0, i * tq:(i + 1) * tq, :] = jnp.broadcast_to(lse, o.shape)


def dilated_group_attention(h3, group, dilation, radius, cos_t, sin_t, seq):
    bsz = h3.shape[0]
    nh = DIL_HEADS
    base = group * 3 * nh
    out_sds = jax.ShapeDtypeStruct((bsz, seq, nh * HEAD_DIM), F32)
    blk = lambda off: pl.BlockSpec((1, seq, LANES), lambda b, h: (b, 0, base + off + h))
    tbl = pl.BlockSpec((seq, LANES), lambda b, h: (0, 0))
    out = pl.BlockSpec((1, seq, LANES), lambda b, h: (b, 0, h))
    o, lse = pl.pallas_call(
        functools.partial(_dil_attn_kernel, seq=seq, dilation=dilation, radius=radius, scale=HEAD_DIM ** -0.5,
                          par=min(dilation, 4)),
        grid=(bsz, nh),
        in_specs=[blk(0), blk(nh), blk(2 * nh), tbl, tbl],
        out_specs=[out, out],
        out_shape=[out_sds, out_sds],
        scratch_shapes=[pltpu.VMEM((seq // dilation, LANES), F32)],
        compiler_params=_params("parallel", "parallel"),
        name=f"dilated_attention_g{group}",
    )(h3, h3, h3, cos_t, sin_t)
    return o.reshape(bsz * seq, nh * HEAD_DIM), lse.reshape(bsz * seq, nh * HEAD_DIM)


def _dil_combine_proj_kernel(*refs, n_groups, alpha):
    w_ref, x_ref, g_ref, b_ref, o_ref = refs[2 * n_groups:]
    outs = [refs[2 * g][...] for g in range(n_groups)]
    lses = [refs[2 * g + 1][...] for g in range(n_groups)]
    m = functools.reduce(jnp.maximum, lses)
    es = [jnp.exp(l - m) for l in lses]
    num = functools.reduce(jnp.add, [e * o for e, o in zip(es, outs)])
    mixed = num / functools.reduce(jnp.add, es)
    acc = alpha * x_ref[...] + jnp.dot(mixed.astype(BF16), w_ref[...], preferred_element_type=F32)
    mu = jnp.mean(acc, axis=-1, keepdims=True)
    zc = acc - mu
    var = jnp.mean(zc * zc, axis=-1, keepdims=True)
    o_ref[...] = zc * lax.rsqrt(var + EPS) * g_ref[...] + b_ref[...]


def dilated_combine_project(pairs, w, x, g, b, alpha, *, tm=256):
    m, k = pairs[0][0].shape
    d = x.shape[1]
    flat = [a for pair in pairs for a in pair]
    spec = pl.BlockSpec((tm, k), lambda i: (i, 0))
    row = pl.BlockSpec((tm, d), lambda i: (i, 0))
    vec = pl.BlockSpec((1, d), lambda i: (0, 0))
    return pl.pallas_call(
        functools.partial(_dil_combine_proj_kernel, n_groups=len(pairs), alpha=alpha),
        grid=(m // tm,),
        in_specs=[spec] * len(flat) + [pl.BlockSpec((k, d), lambda i: (0, 0)), row, vec, vec],
        out_specs=row,
        out_shape=jax.ShapeDtypeStruct((m, d), F32),
        compiler_params=_params("parallel"),
        name="dilated_combine_project",
    )(*flat, w, x, g.reshape(1, d), b.reshape(1, d))


def _top_rows(s, k, payload=None):
    rows = s.shape[0]
    idx = lax.broadcasted_iota(jnp.int32, s.shape, 0).astype(F32)
    vals, inds, pays = [], [], []
    for _ in range(k):
        m = jnp.max(s, axis=0, keepdims=True)
        sel = jnp.min(jnp.where(s == m, idx, float(rows)), axis=0, keepdims=True)
        hit = idx == sel
        vals.append(m)
        inds.append(sel)
        if payload is not None:
            pays.append(jnp.sum(jnp.where(hit, payload, 0.0), axis=0, keepdims=True))
        s = jnp.where(hit, -jnp.inf, s)
    cat = lambda xs: jnp.concatenate(xs, axis=0)
    return cat(vals), cat(inds), (cat(pays) if payload is not None else None)


def _peer_select_kernel(qt_ref, keys_ref, e_ref, g_ref):
    kk = PEER_TOPK
    half = PEER_QUERY_DIM // 2
    for lb in range(qt_ref.shape[1] // LANES):
        lanes = slice(lb * LANES, (lb + 1) * LANES)
        tops = []
        for p in range(2):
            s = jnp.dot(keys_ref[p], qt_ref[p * half:(p + 1) * half, lanes], preferred_element_type=F32)
            tops.append(_top_rows(s, kk)[:2])
        (s0, i0), (s1, i1) = tops
        sub = lax.broadcasted_iota(jnp.int32, (8, LANES), 0)
        pieces_s, pieces_e = [], []
        for a in range(kk // 2):
            nb = kk // (a + 1)
            rows = kk if nb > 8 else 8
            ps = s0[a:a + 1] + s1[0:rows]
            if nb < rows:
                ps = jnp.where(sub < nb, ps, -jnp.inf)
            pieces_s.append(ps)
            pieces_e.append(i0[a:a + 1] * float(PEER_N_KEYS) + i1[0:rows])
        pieces_s.append(s0[kk // 2:kk] + s1[0:1])
        pieces_e.append(i0[kk // 2:kk] * float(PEER_N_KEYS) + i1[0:1])
        cand_s = jnp.concatenate(pieces_s, axis=0)
        cand_e = jnp.concatenate(pieces_e, axis=0)
        best_s, _, experts = _top_rows(cand_s, kk, payload=cand_e)
        ex = jnp.exp(best_s - best_s[0:1])
        e_ref[0, :, lanes] = experts.astype(jnp.int32)
        g_ref[0, :, lanes] = ex / jnp.sum(ex, axis=0, keepdims=True)


def peer_select(q_t, sub_keys, *, tt=512):
    t = q_t.shape[1]
    nh = PEER_HEADS
    kk = PEER_TOPK
    return pl.pallas_call(
        _peer_select_kernel,
        grid=(t // tt, nh),
        in_specs=[
            pl.BlockSpec((PEER_QUERY_DIM, tt), lambda i, h: (h, i)),
            pl.BlockSpec((2, PEER_N_KEYS, PEER_QUERY_DIM // 2), lambda i, h: (0, 0, 0)),
        ],
        out_specs=[
            pl.BlockSpec((1, kk, tt), lambda i, h: (h, 0, i)),
            pl.BlockSpec((1, kk, tt), lambda i, h: (h, 0, i)),
        ],
        out_shape=[jax.ShapeDtypeStruct((nh, kk, t), jnp.int32), jax.ShapeDtypeStruct((nh, kk, t), F32)],
        compiler_params=_params("parallel", "parallel"),
        name="peer_select",
    )(q_t, sub_keys)


def _erf(x):
    x = jnp.clip(x, -4.0, 4.0)
    x2 = x * x
    alpha = (-2.72614225801306e-10, 2.77068142495902e-08, -2.10102402082508e-06, -5.69250639462346e-05,
             -7.34990630326855e-04, -2.95459980854025e-03, -1.60960333262415e-02)
    beta = (-1.45660718464996e-05, -2.13374055278905e-04, -1.68282697438203e-03, -7.37332916720468e-03,
            -1.42647390514189e-02)
    p = jnp.full_like(x, alpha[0])
    for c in alpha[1:]:
        p = p * x2 + c
    q = jnp.full_like(x, beta[0])
    for c in beta[1:]:
        q = q * x2 + c
    return x * p / q


PEER_SLOTS = 8


def _peer_expert_kernel(idx_hbm, g_ref, x_ref, uv_hbm, lng_ref, lnb_ref, y_ref, idx_smem, buf, isem, sem, *, tb, d, alpha):
    n_sel = PEER_SEL
    n_chunks = d // LANES
    ns = PEER_SLOTS
    step = pl.program_id(0)
    idx_copy = pltpu.make_async_copy(idx_hbm.at[pl.ds(step * (tb * n_sel), tb * n_sel)], idx_smem, isem)
    idx_copy.start()
    idx_copy.wait()

    def row_copy(slot, row, j):
        return pltpu.make_async_copy(uv_hbm.at[row], buf.at[slot, :, j, :], sem.at[slot])

    def issue(t, slot):
        base = t * n_sel
        for j in range(n_sel):
            row_copy(slot, idx_smem[base + j], j).start(priority=j % 2)

    def wait(slot):
        for j in range(n_sel):
            row_copy(slot, 0, j).wait()

    ii = lax.broadcasted_iota(jnp.int32, (n_sel, n_sel), 0)
    jj = lax.broadcasted_iota(jnp.int32, (n_sel, n_sel), 1)
    eye = ii == jj
    high = jnp.uint32(0xFFFF0000)

    def consume(t, slot):
        x_row = x_ref[pl.ds(t, 1), :]
        acc = jnp.zeros((n_sel, LANES), F32)
        for c in range(n_chunks):
            u_c = lax.bitcast_convert_type(buf[slot, c] & high, F32)
            acc += u_c * x_row[:, c * LANES:(c + 1) * LANES]
        hid = jnp.sum(acc, axis=1, keepdims=True)
        g_col = jnp.sum(jnp.where(eye, g_ref[pl.ds(t, 1), :], 0.0), axis=1, keepdims=True)
        act = g_col * (0.5 * hid * (1.0 + _erf(hid * (2.0 ** -0.5))))
        y_chunks = []
        for c in range(n_chunks):
            v_c = lax.bitcast_convert_type(buf[slot, c] << 16, F32)
            y_chunks.append(jnp.sum(v_c * act, axis=0, keepdims=True))
        y_ref[pl.ds(t, 1), :] = jnp.concatenate(y_chunks, axis=1)

    for t0 in range(ns - 1):
        issue(t0, t0)

    def group(k, carry):
        for j in range(ns):
            t = k * ns + j
            issue(t + ns - 1, (j + ns - 1) % ns)
            wait(j)
            consume(t, j)
        return carry

    lax.fori_loop(0, tb // ns - 1, group, 0)
    for j in range(ns):
        t = tb - ns + j
        if j == 0:
            issue(tb - 1, ns - 1)
        wait(j)
        consume(t, j)
    z = alpha * x_ref[...] + y_ref[...]
    mu = jnp.mean(z, axis=-1, keepdims=True)
    zc = z - mu
    var = jnp.mean(zc * zc, axis=-1, keepdims=True)
    y_ref[...] = zc * lax.rsqrt(var + EPS) * lng_ref[...] + lnb_ref[...]


def peer_experts(idx, gates, x, uv, ln_g, ln_b, alpha, *, tb=512):
    t, d = x.shape
    n_sel = PEER_SEL
    return pl.pallas_call(
        functools.partial(_peer_expert_kernel, tb=tb, d=d, alpha=alpha),
        grid=(t // tb,),
        in_specs=[
            pl.BlockSpec(memory_space=pl.ANY),
            pl.BlockSpec((tb, n_sel), lambda i: (i, 0)),
            pl.BlockSpec((tb, d), lambda i: (i, 0)),
            pl.BlockSpec(memory_space=pl.ANY),
            pl.BlockSpec((1, d), lambda i: (0, 0)),
            pl.BlockSpec((1, d), lambda i: (0, 0)),
        ],
        out_specs=pl.BlockSpec((tb, d), lambda i: (i, 0)),
        out_shape=jax.ShapeDtypeStruct((t, d), F32),
        scratch_shapes=[
            pltpu.SMEM((tb * n_sel,), jnp.int32),
            pltpu.VMEM((PEER_SLOTS, d // LANES, n_sel, LANES), jnp.uint32),
            pltpu.SemaphoreType.DMA,
            pltpu.SemaphoreType.DMA((PEER_SLOTS,)),
        ],
        compiler_params=_params("arbitrary"),
        name="peer_experts",
    )(idx, gates, x, uv, ln_g.reshape(1, d), ln_b.reshape(1, d))


def _pack_expert_tables(u, v):
    e, d = u.shape
    hi = lax.bitcast_convert_type(u.astype(BF16), jnp.uint16).astype(jnp.uint32) << 16
    lo = lax.bitcast_convert_type(v.astype(BF16), jnp.uint16).astype(jnp.uint32)
    return (hi | lo).reshape(e, d // LANES, LANES)


def peer_layer(x, w_q, sub_keys, u, v, ln_g, ln_b, alpha):
    t = x.shape[0]
    q_t = matmul_nt(w_q.T.astype(BF16), x)
    experts, gates = peer_select(q_t, sub_keys)
    idx = experts.reshape(PEER_SEL, t).T.reshape(t * PEER_SEL)
    gates = gates.reshape(PEER_SEL, t).T
    return peer_experts(idx, gates, x, _pack_expert_tables(u, v), ln_g, ln_b, alpha)


def _even_mixer(xt, bsz, seq, w_in, conv_w, gate_params, norm_g, q_norm_g, w_uq, kv_norm_g, w_ukv, w_out,
                cos_m, sin_m):
    nh = DN_HEADS
    c_qkvz = 4 * nh * DN_DK
    c_gate = c_qkvz + 4 * nh
    c_cq = c_gate + MLA_Q_RANK
    c_ckv = c_cq + MLA_KV_RANK
    w_gate = w_in[:, c_qkvz:c_gate]
    w_kr = w_in[:, c_ckv:]
    w_all = jnp.concatenate(
        [w_in[:, :c_qkvz], w_in[:, c_gate:c_ckv], w_kr, w_kr, w_gate,
         jnp.zeros((w_in.shape[0], LANES - 4 * nh), w_in.dtype)], axis=1).astype(BF16)
    h_all = matmul(xt, w_all, tn=768)
    cols = w_all.shape[1]
    h3d = h_all.reshape(bsz, seq, cols)
    lat0 = c_qkvz
    kpe_block = (lat0 + MLA_Q_RANK + MLA_KV_RANK) // LANES
    g0 = lat0 + MLA_Q_RANK + MLA_KV_RANK + LANES

    qkv = deltanet_conv(h3d, conv_w, seq)
    n_chunks = seq // DN_CHUNK
    gates = h_all[:, g0:g0 + 4 * nh].reshape(bsz, n_chunks, DN_CHUNK, 4, nh).transpose(0, 4, 3, 1, 2)
    o2 = deltanet_scan(qkv, gates, gate_params, seq)
    o_a = deltanet_gated_norm(o2, h3d, 3 * nh, norm_g, seq)

    hq = MLA_NOPE + MLA_ROPE
    w_uq_r = w_uq.reshape(MLA_Q_RANK, MLA_HEADS, hq)
    w_uq_p = jnp.concatenate([w_uq_r[:, :, :MLA_NOPE].reshape(MLA_Q_RANK, -1),
                              w_uq_r[:, :, MLA_NOPE:].reshape(MLA_Q_RANK, -1)], axis=1).astype(BF16)
    w_ukv_r = w_ukv.reshape(MLA_KV_RANK, MLA_HEADS, MLA_NOPE + MLA_V)
    w_ukv_p = jnp.concatenate([w_ukv_r[:, :, :MLA_NOPE].reshape(MLA_KV_RANK, -1),
                               w_ukv_r[:, :, MLA_NOPE:].reshape(MLA_KV_RANK, -1)], axis=1).astype(BF16)
    q_up = rms_matmul(h_all, lat0 // MLA_Q_RANK, q_norm_g, w_uq_p)
    kv_up = rms_matmul(h_all, (lat0 + MLA_Q_RANK) // MLA_KV_RANK, kv_norm_g, w_ukv_p)
    o_b = mla_attention(q_up.reshape(bsz, seq, -1), kv_up.reshape(bsz, seq, -1), h3d, kpe_block, cos_m, sin_m, seq)

    n_a = nh * DN_DV
    w_out = w_out.astype(BF16)
    return [(o_a.reshape(bsz * seq, n_a), w_out[:n_a]), (o_b.reshape(bsz * seq, -1), w_out[n_a:])]


def _odd_layer(xt, bsz, seq, w_in, w_out, cos_h, sin_h, ln_g, ln_b, alpha):
    h3 = matmul(xt, w_in.astype(BF16), tn=768).reshape(bsz, seq, -1)
    pairs = []
    for g, (window, dilation) in enumerate(DIL_GROUPS):
        pairs.append(dilated_group_attention(h3, g, dilation, window // (2 * dilation), cos_h, sin_h, seq))
    return dilated_combine_project(pairs, w_out.astype(BF16), xt, ln_g, ln_b, alpha)


def kernel(x, ab_w_in, a_conv_w, a_log_f, a_dt_bias_f, a_log_b, a_dt_bias_b, a_out_norm_g, b_q_norm_g, b_w_uq,
           b_kv_norm_g, b_w_ukv, ab_w_out, c_w_in, c_w_out, mix_ln_g, mix_ln_b, peer_w_q, peer_sub_keys, peer_u,
           peer_v, ffn_ln_g, ffn_ln_b):
    bsz, seq, d = x.shape
    depth = mix_ln_g.shape[0]
    alpha = (2 * depth) ** 0.25
    cos_h, sin_h = _rope_tables(seq, HEAD_DIM)
    cos_m, sin_m = _rope_tables(seq, MLA_ROPE)
    xt = x.reshape(bsz * seq, d)
    for layer in range(depth):
        i = layer // 2
        if layer % 2 == 0:
            gate_params = jnp.stack([a_log_f[i], a_dt_bias_f[i], a_log_b[i], a_dt_bias_b[i]], axis=1)
            mix = _even_mixer(xt, bsz, seq, ab_w_in[i], a_conv_w[i], gate_params, a_out_norm_g[i], b_q_norm_g[i],
                              b_w_uq[i], b_kv_norm_g[i], b_w_ukv[i], ab_w_out[i], cos_m, sin_m)
            xt = matmul_add_layer_norm(mix, xt, mix_ln_g[layer], mix_ln_b[layer], alpha)
        else:
            xt = _odd_layer(xt, bsz, seq, c_w_in[i], c_w_out[i], cos_h, sin_h, mix_ln_g[layer], mix_ln_b[layer],
                            alpha)
        xt = peer_layer(xt, peer_w_q[layer], peer_sub_keys[layer], peer_u[layer], peer_v[layer],
                        ffn_ln_g[layer], ffn_ln_b[layer], alpha)
    return xt.reshape(bsz, seq, d)
```
